```python
import math
import jax
import jax.numpy as jnp
from jax import lax
import numpy as np

D_MODEL = 2048
BATCH = 4
SEQ = 2048
DEPTH = 2

GRID_W = 64
CTX_LEN = 256
D_FF = 5632
SHORT_CONV = 5
Q_BLOCK = 128
ROPE_THETA = 10000.0
N_MOD = 9
DN_HEADS = 4
DN_DK = 128
DN_DV = 128
DN_CHUNK = 64
GQA_HEADS = 4
GQA_KV_HEADS = 2
GQA_DIM = 128
SSD_HEADS = 8
SSD_HEADDIM = 64
SSD_GROUPS = 2
SSD_STATE = 128
SSD_CHUNK = 128
DIFF_HEADS = 4
DIFF_DIM = 64
DIFF_VDIM = 128

DN_W = DN_HEADS * DN_DV
GQA_W = GQA_HEADS * GQA_DIM
SSD_W = SSD_HEADS * SSD_HEADDIM
DIFF_W = DIFF_HEADS * DIFF_VDIM
D_MIX = DN_W + GQA_W + SSD_W + DIFF_W
DN_CONV_CH = 2 * DN_HEADS * DN_DK + DN_W
DN_IN = DN_CONV_CH + DN_W + 4 * DN_HEADS
GQA_IN = (GQA_HEADS + 2 * GQA_KV_HEADS) * GQA_DIM
SSD_CONV_CH = SSD_W + 2 * SSD_GROUPS * SSD_STATE
SSD_IN = SSD_W + SSD_CONV_CH + 2 * SSD_HEADS
DIFF_IN = 4 * DIFF_HEADS * DIFF_DIM + DIFF_W
D_IN = DN_IN + GQA_IN + SSD_IN + DIFF_IN
ALPHA = (2 * DEPTH) ** 0.25
BETA_INIT = (8 * DEPTH) ** -0.25

kernel_name = 'hybrid_head_group_dit_block'


def _layernorm(x, g, b, eps=1e-5):
    xf = x.astype(jnp.float32)
    mu = jnp.mean(xf, axis=-1, keepdims=True)
    var = jnp.mean(jnp.square(xf - mu), axis=-1, keepdims=True)
    return ((xf - mu) * lax.rsqrt(var + eps) * g + b).astype(x.dtype)


def _rmsnorm(x, g, eps=1e-6):
    xf = x.astype(jnp.float32)
    return (xf * lax.rsqrt(jnp.mean(jnp.square(xf), axis=-1, keepdims=True) + eps) * g).astype(x.dtype)


def _l2norm(x, eps=1e-6):
    xf = x.astype(jnp.float32)
    return (xf * lax.rsqrt(jnp.sum(jnp.square(xf), axis=-1, keepdims=True) + eps)).astype(x.dtype)


def _modulate(h, shift, scale):
    return h * (1.0 + scale) + shift


def _post_norm(h, delta, g, b):
    return _layernorm(ALPHA * h + delta, g, b)


def _half_ffn(h, shift, scale, gate, wg, wu, wd):
    hm = _modulate(h, shift, scale)
    return 0.5 * gate * ((jax.nn.silu(hm @ wg) * (hm @ wu)) @ wd)


def _split(t, sizes):
    return jnp.split(t, [int(s) for s in np.cumsum(sizes)[:-1]], axis=-1)


def _heads(t, n_heads):
    b, s, _ = t.shape
    return t.reshape(b, s, n_heads, -1).transpose(0, 2, 1, 3)


def _merge(t):
    b, h, s, d = t.shape
    return t.transpose(0, 2, 1, 3).reshape(b, s, h * d)


def _centred_dwconv(x, w):
    return lax.conv_general_dilated(x, w[:, None, :].astype(x.dtype), window_strides=(1,), padding='SAME',
                                    dimension_numbers=('NWC', 'WIO', 'NWC'), feature_group_count=x.shape[-1])


def _axial_rope_tables(n_tok, head_dim):
    rows = n_tok // GRID_W
    row = jnp.broadcast_to(jnp.arange(rows)[:, None], (rows, GRID_W)).reshape(-1).astype(jnp.float32)
    col = jnp.broadcast_to(jnp.arange(GRID_W)[None, :], (rows, GRID_W)).reshape(-1).astype(jnp.float32)
    half = head_dim // 2
    inv = ROPE_THETA ** (-jnp.arange(0, half, 2, dtype=jnp.float32) / half)
    ang_r = row[:, None] * inv
    ang_c = col[:, None] * inv
    return (jnp.cos(ang_r), jnp.sin(ang_r), jnp.cos(ang_c), jnp.sin(ang_c))


def _rotate(x, cos, sin):
    x1, x2 = jnp.split(x, 2, axis=-1)
    return jnp.concatenate([x1 * cos - x2 * sin, x1 * sin + x2 * cos], axis=-1).astype(x.dtype)


def _axial_rope(x, tabs):
    cr, sr, cc, sc = tabs
    xr, xc = jnp.split(x, 2, axis=-1)
    return jnp.concatenate([_rotate(xr, cr, sr), _rotate(xc, cc, sc)], axis=-1)


def _sweep_query_blocks(fn, q):
    *lead, t, d = q.shape
    nb = t // Q_BLOCK
    qb = jnp.moveaxis(q.reshape(*lead, nb, Q_BLOCK, d), -3, 0)
    o = jnp.moveaxis(lax.map(fn, qb), 0, -3)
    return o.reshape(*o.shape[:-3], t, o.shape[-1])


def _bidirectional(scan_fn, ctx_dirs, lat_dirs, s0, need_ctx):
    y_lat, y_ctx = 0, 0
    for d in range(2):
        f = (lambda t: jnp.flip(t, axis=2)) if d == 1 else (lambda t: t)
        yc, sc = scan_fn(*[f(t) for t in ctx_dirs[d]], s0)
        yl, _ = scan_fn(*[f(t) for t in lat_dirs[d]], sc)
        y_lat = y_lat + f(yl)
        if need_ctx:
            y_ctx = y_ctx + f(yc)
    return y_lat, (y_ctx if need_ctx else None)


def _unit_lower_inverse(a):
    c = a.shape[-1]
    eye = jnp.eye(c, dtype=a.dtype)
    m = -a
    inv = eye + m
    for _ in range(int(math.log2(c)) - 1):
        m = m @ m
        inv = inv @ (eye + m)
    return inv


def _gated_delta_chunked(q, k, v, g, beta, s0):
    out_dtype = v.dtype
    q, k, v = (t.astype(jnp.float32) for t in (q, k, v))
    b, h, t, dk = q.shape
    dv = v.shape[-1]
    c = DN_CHUNK
    n = t // c
    q = q.reshape(b, h, n, c, dk)
    k = k.reshape(b, h, n, c, dk)
    v = v.reshape(b, h, n, c, dv)
    g = jnp.cumsum(g.reshape(b, h, n, c), axis=-1)
    beta = beta.reshape(b, h, n, c)
    incl = jnp.tril(jnp.ones((c, c), bool))
    strict = jnp.tril(jnp.ones((c, c), bool), -1)
    seg = g[..., :, None] - g[..., None, :]
    decay = jnp.where(incl, jnp.exp(jnp.where(incl, seg, 0.0)), 0.0)
    kb = k * beta[..., None]
    a = jnp.where(strict, jnp.einsum('bhcik,bhcjk->bhcij', kb, k) * decay, 0.0)
    t_inv = _unit_lower_inverse(a)
    u = t_inv @ (v * beta[..., None])
    w = t_inv @ (kb * jnp.exp(g)[..., None])
    qk = jnp.einsum('bhcik,bhcjk->bhcij', q, k) * decay
    qg = q * jnp.exp(g)[..., None]
    kd = k * jnp.exp(g[..., -1:] - g)[..., None]
    g_last = jnp.exp(g[..., -1])

    def step(s, xs):
        u_c, w_c, qk_c, qg_c, kd_c, gl_c = xs
        v_new = u_c - w_c @ s
        o = qg_c @ s + qk_c @ v_new
        s = s * gl_c[..., None, None] + jnp.einsum('bhck,bhcv->bhkv', kd_c, v_new)
        return s, o

    xs = tuple(jnp.moveaxis(z, 2, 0) for z in (u, w, qk, qg, kd, g_last))
    s_fin, o = lax.scan(step, s0, xs)
    return jnp.moveaxis(o, 0, 2).reshape(b, h, t, dv).astype(out_dtype), s_fin


def _gated_deltanet(pl, pc, conv_w, a_log, dt_bias, norm_g, need_ctx):
    hh = DN_HEADS

    def prep(p):
        b, t, _ = p.shape
        qkv, z, bb, aa = _split(p, (DN_CONV_CH, DN_W, 2 * hh, 2 * hh))
        qkv = jax.nn.silu(_centred_dwconv(qkv, conv_w))
        q, k, v = _split(qkv, (hh * DN_DK, hh * DN_DK, DN_W))
        q = _l2norm(_heads(q, hh)) * DN_DK ** -0.5
        k = _l2norm(_heads(k, hh))
        v = _heads(v, hh)
        beta = jax.nn.sigmoid(bb.astype(jnp.float32))
        g = -jnp.exp(a_log.astype(jnp.float32).reshape(-1)) * jax.nn.softplus(aa.astype(jnp.float32) + dt_bias.reshape(-1))
        beta = beta.reshape(b, t, 2, hh).transpose(2, 0, 3, 1)
        g = g.reshape(b, t, 2, hh).transpose(2, 0, 3, 1)
        return [(q, k, v, g[d], beta[d]) for d in range(2)], z

    dirs_l, zl = prep(pl)
    dirs_c, zc = prep(pc)
    s0 = jnp.zeros((pl.shape[0], hh, DN_DK, DN_DV), jnp.float32)
    ol, oc = _bidirectional(_gated_delta_chunked, dirs_c, dirs_l, s0, need_ctx)
    yl = _merge(_rmsnorm(ol, norm_g)) * jax.nn.silu(zl)
    yc = _merge(_rmsnorm(oc, norm_g)) * jax.nn.silu(zc) if need_ctx else None
    return yl, yc


def _gqa_axial(pl, pc, q_norm, k_norm, rope, need_ctx):
    hq, hk, dh = GQA_HEADS, GQA_KV_HEADS, GQA_DIM
    r = hq // hk

    def prep(p):
        q, k, v = _split(p, (hq * dh, hk * dh, hk * dh))
        return _rmsnorm(_heads(q, hq), q_norm), _rmsnorm(_heads(k, hk), k_norm), _heads(v, hk)

    ql, kl, vl = prep(pl)
    qc, kc, vc = prep(pc)
    ql = _axial_rope(ql, rope)
    kl = _axial_rope(kl, rope)
    k_all = jnp.concatenate([kc, kl], axis=2)
    v_all = jnp.concatenate([vc, vl], axis=2)

    def attend(keys, vals):
        def fn(qb):
            s = jnp.einsum('bgrqd,bgkd->bgrqk', qb, keys).astype(jnp.float32) * dh ** -0.5
            p = jax.nn.softmax(s, axis=-1).astype(vals.dtype)
            return jnp.einsum('bgrqk,bgkd->bgrqd', p, vals)
        return fn

    b, _, tl, _ = ql.shape
    ol = _sweep_query_blocks(attend(k_all, v_all), ql.reshape(b, hk, r, tl, dh)).reshape(b, hq, tl, dh)
    if not need_ctx:
        return _merge(ol), None
    tc = qc.shape[2]
    oc = attend(kc, vc)(qc.reshape(b, hk, r, tc, dh)).reshape(b, hq, tc, dh)
    return _merge(ol), _merge(oc)


def _ssd_chunked(x, dt, da, bm, cm, s0):
    out_dtype = x.dtype
    x, bm, cm = (t.astype(jnp.float32) for t in (x, bm, cm))
    b, h, t, p = x.shape
    g, nst = bm.shape[1], bm.shape[-1]
    r = h // g
    ll = SSD_CHUNK
    n = t // ll
    xg = x.reshape(b, g, r, n, ll, p)
    dtg = dt.reshape(b, g, r, n, ll)
    cum = jnp.cumsum(da.reshape(b, g, r, n, ll), axis=-1)
    bc = bm.reshape(b, g, n, ll, nst)
    cc = cm.reshape(b, g, n, ll, nst)
    tril = jnp.tril(jnp.ones((ll, ll), bool))
    seg = cum[..., :, None] - cum[..., None, :]
    decay = jnp.where(tril, jnp.exp(jnp.where(tril, seg, 0.0)), 0.0)
    xdt = xg * dtg[..., None]
    scores = jnp.einsum('bgcln,bgcsn->bgcls', cc, bc)
    y_diag = jnp.einsum('bgcls,bgrcls,bgrcsp->bgrclp', scores, decay, xdt)
    w_state = jnp.exp(cum[..., -1:] - cum)
    contrib = jnp.einsum('bgcsn,bgrcs,bgrcsp->bgrcpn', bc, w_state, xdt)
    chunk_decay = jnp.exp(cum[..., -1])
    c_in = jnp.exp(cum)

    def step(s, xs):
        cont, cd, cc_c, cin_c = xs
        y_off = jnp.einsum('bgln,bgrl,bgrpn->bgrlp', cc_c, cin_c, s)
        s = s * cd[..., None, None] + cont
        return s, y_off

    xs = (jnp.moveaxis(contrib, 3, 0), jnp.moveaxis(chunk_decay, 3, 0), jnp.moveaxis(cc, 2, 0), jnp.moveaxis(c_in, 3, 0))
    s_fin, y_off = lax.scan(step, s0, xs)
    y = y_diag + jnp.moveaxis(y_off, 0, 3)
    return y.reshape(b, h, t, p).astype(out_dtype), s_fin


def _ssd_bidir(pl, pc, conv_w, conv_b, a_log, dt_bias, d_skip, norm_g, need_ctx):
    hh, g = SSD_HEADS, SSD_GROUPS

    def prep(p):
        b, t, _ = p.shape
        z, xbc, dt = _split(p, (SSD_W, SSD_CONV_CH, 2 * hh))
        xbc = jax.nn.silu(_centred_dwconv(xbc, conv_w) + conv_b)
        xs, bm, cm = _split(xbc, (SSD_W, g * SSD_STATE, g * SSD_STATE))
        xs = _heads(xs, hh)
        bm = _heads(bm, g)
        cm = _heads(cm, g)
        dt = jax.nn.softplus(dt.astype(jnp.float32) + dt_bias.reshape(-1))
        dt = dt.reshape(b, t, 2, hh).transpose(2, 0, 3, 1)
        da = dt * (-jnp.exp(a_log.astype(jnp.float32)))[:, None, :, None]
        return [(xs, dt[d], da[d], bm, cm) for d in range(2)], xs, z

    def finish(y, xs, z):
        y = _merge(y + d_skip[:, None, None] * xs) * jax.nn.silu(z)
        b, t, _ = y.shape
        y = _rmsnorm(y.reshape(b, t, g, SSD_W // g), norm_g.reshape(g, SSD_W // g))
        return y.reshape(b, t, SSD_W)

    dirs_l, xl, zl = prep(pl)
    dirs_c, xc, zc = prep(pc)
    s0 = jnp.zeros((pl.shape[0], g, hh // g, SSD_HEADDIM, SSD_STATE), jnp.float32)
    yl, yc = _bidirectional(_ssd_chunked, dirs_c, dirs_l, s0, need_ctx)
    return finish(yl, xl, zl), (finish(yc, xc, zc) if need_ctx else None)


def _diff_attention(pl, pc, lam_vecs, norm_g, lam_init, rope, need_ctx):
    hh = DIFF_HEADS

    def prep(p):
        b, t, _ = p.shape
        q, k, v = _split(p, (2 * hh * DIFF_DIM, 2 * hh * DIFF_DIM, DIFF_W))
        q = _heads(q, 2 * hh).reshape(b, hh, 2, t, DIFF_DIM)
        k = _heads(k, 2 * hh).reshape(b, hh, 2, t, DIFF_DIM)
        return q, k, _heads(v, hh)

    lv = lam_vecs.astype(jnp.float32)
    lam = jnp.exp(jnp.sum(lv[0] * lv[1])) - jnp.exp(jnp.sum(lv[2] * lv[3])) + lam_init
    ql, kl, vl = prep(pl)
    qc, kc, vc = prep(pc)
    ql = _axial_rope(ql, rope)
    kl = _axial_rope(kl, rope)
    k_all = jnp.concatenate([kc, kl], axis=3)
    v_all = jnp.concatenate([vc, vl], axis=2)

    def attend(keys, vals):
        def fn(qb):
            s = jnp.einsum('bhiqd,bhikd->bhiqk', qb, keys).astype(jnp.float32) * DIFF_DIM ** -0.5
            p = jax.nn.softmax(s, axis=-1)
            a = p[:, :, 0] - lam * p[:, :, 1]
            return jnp.einsum('bhqk,bhkd->bhqd', a.astype(vals.dtype), vals)
        return fn

    def finish(o):
        return _merge(_rmsnorm(o, norm_g) * (1.0 - lam_init))

    ol = _sweep_query_blocks(attend(k_all, v_all), ql)
    if not need_ctx:
        return finish(ol), None
    return finish(ol), finish(attend(kc, vc)(qc))


def _token_mixing(hl, hc, w_in, w_out, dn_conv, dn_a_log, dn_dt_bias, dn_norm_g, gqa_q_norm, gqa_k_norm,
                  ssd_conv, ssd_conv_b, ssd_a_log, ssd_dt_bias, ssd_d, ssd_norm_g, diff_lambda, diff_norm_g,
                  lam_init, rope_gqa, rope_diff, need_ctx):
    sizes = (DN_IN, GQA_IN, SSD_IN, DIFF_IN)
    al, bl, cl, dl = _split(hl @ w_in, sizes)
    ac, bc, cc, dc = _split(hc @ w_in, sizes)
    outs = [
        _gated_deltanet(al, ac, dn_conv, dn_a_log, dn_dt_bias, dn_norm_g, need_ctx),
        _gqa_axial(bl, bc, gqa_q_norm, gqa_k_norm, rope_gqa, need_ctx),
        _ssd_bidir(cl, cc, ssd_conv, ssd_conv_b, ssd_a_log, ssd_dt_bias, ssd_d, ssd_norm_g, need_ctx),
        _diff_attention(dl, dc, diff_lambda, diff_norm_g, lam_init, rope_diff, need_ctx),
    ]
    yl = jnp.concatenate([o[0] for o in outs], axis=-1) @ w_out
    yc = jnp.concatenate([o[1] for o in outs], axis=-1) @ w_out if need_ctx else None
    return yl, yc


def setup_inputs(seed: int = 0) -> dict:
    key = jax.random.key(seed)
    ks = jax.random.split(key, 28)
    f32 = jnp.float32
    L, D = DEPTH, D_MODEL

    def nrm(k, shape, s):
        return jax.random.normal(k, shape, f32) * s

    def log_a(k, shape):
        return jnp.log(jax.random.uniform(k, shape, f32, 1.0, 16.0))

    def dt_bias(k, shape):
        dt = jnp.exp(jax.random.uniform(k, shape, f32, math.log(1e-3), math.log(1e-1)))
        return dt + jnp.log(-jnp.expm1(-dt))

    return {
        'x': nrm(ks[0], (BATCH, SEQ, D), 1.0),
        'c': nrm(ks[1], (BATCH, D), 1.0),
        'ctx': nrm(ks[2], (BATCH, CTX_LEN, D), 1.0),
        'c_ctx': nrm(ks[3], (D,), 1.0),
        'w_mod': nrm(ks[4], (L, D, N_MOD * D), 0.5 * D ** -0.5),
        'b_mod': nrm(ks[5], (L, N_MOD * D), 0.02),
        'ln_g': 1.0 + nrm(ks[6], (L, 3, D), 0.05),
        'ln_b': nrm(ks[7], (L, 3, D), 0.02),
        'ffn_wg': nrm(ks[8], (L, 2, D, D_FF), D ** -0.5),
        'ffn_wu': nrm(ks[9], (L, 2, D, D_FF), D ** -0.5),
        'ffn_wd': nrm(ks[10], (L, 2, D_FF, D), BETA_INIT * D_FF ** -0.5),
        'w_in': nrm(ks[11], (L, D, D_IN), D ** -0.5),
        'w_out': nrm(ks[12], (L, D_MIX, D), BETA_INIT * D_MIX ** -0.5),
        'dn_conv': nrm(ks[13], (L, SHORT_CONV, DN_CONV_CH), SHORT_CONV ** -0.5),
        'dn_a_log': log_a(ks[14], (L, 2, DN_HEADS)),
        'dn_dt_bias': dt_bias(ks[15], (L, 2, DN_HEADS)),
        'dn_norm_g': 1.0 + nrm(ks[16], (L, DN_DV), 0.05),
        'gqa_q_norm': 1.0 + nrm(ks[17], (L, GQA_DIM), 0.05),
        'gqa_k_norm': 1.0 + nrm(ks[18], (L, GQA_DIM), 0.05),
        'ssd_conv': nrm(ks[19], (L, SHORT_CONV, SSD_CONV_CH), SHORT_CONV ** -0.5),
        'ssd_conv_b': nrm(ks[20], (L, SSD_CONV_CH), 0.02),
        'ssd_a_log': log_a(ks[21], (L, 2, SSD_HEADS)),
        'ssd_dt_bias': dt_bias(ks[22], (L, 2, SSD_HEADS)),
        'ssd_d': 1.0 + nrm(ks[23], (L, SSD_HEADS), 0.1),
        'ssd_norm_g': 1.0 + nrm(ks[24], (L, SSD_W), 0.05),
        'diff_lambda': nrm(ks[25], (L, 4, DIFF_DIM), 0.1),
        'diff_norm_g': 1.0 + nrm(ks[26], (L, DIFF_VDIM), 0.05),
    }


def reference(x, c, ctx, c_ctx, w_mod, b_mod, ln_g, ln_b, ffn_wg, ffn_wu, ffn_wd, w_in, w_out,
              dn_conv, dn_a_log, dn_dt_bias, dn_norm_g, gqa_q_norm, gqa_k_norm,
              ssd_conv, ssd_conv_b, ssd_a_log, ssd_dt_bias, ssd_d, ssd_norm_g,
              diff_lambda, diff_norm_g):
    n_lat = x.shape[1]
    rope_gqa = _axial_rope_tables(n_lat, GQA_DIM)
    rope_diff = _axial_rope_tables(n_lat, DIFF_DIM)
    s_lat = jax.nn.silu(c)[:, None, :]
    s_ctx = jax.nn.silu(c_ctx)[None, None, :]
    xl, xc = x, ctx
    for l in range(DEPTH):
        need_ctx = l < DEPTH - 1
        lam_init = 0.8 - 0.6 * math.exp(-0.3 * l)
        ml = jnp.split(s_lat @ w_mod[l] + b_mod[l], N_MOD, axis=-1)
        mc = jnp.split(s_ctx @ w_mod[l] + b_mod[l], N_MOD, axis=-1)
        ffn1 = (ffn_wg[l, 0], ffn_wu[l, 0], ffn_wd[l, 0])
        ffn2 = (ffn_wg[l, 1], ffn_wu[l, 1], ffn_wd[l, 1])
        xl = _post_norm(xl, _half_ffn(xl, *ml[0:3], *ffn1), ln_g[l, 0], ln_b[l, 0])
        xc = _post_norm(xc, _half_ffn(xc, *mc[0:3], *ffn1), ln_g[l, 0], ln_b[l, 0])
        yl, yc = _token_mixing(_modulate(xl, ml[3], ml[4]), _modulate(xc, mc[3], mc[4]), w_in[l], w_out[l],
                               dn_conv[l], dn_a_log[l], dn_dt_bias[l], dn_norm_g[l], gqa_q_norm[l], gqa_k_norm[l],
                               ssd_conv[l], ssd_conv_b[l], ssd_a_log[l], ssd_dt_bias[l], ssd_d[l], ssd_norm_g[l],
                               diff_lambda[l], diff_norm_g[l], lam_init, rope_gqa, rope_diff, need_ctx)
        xl = _post_norm(xl, ml[5] * yl, ln_g[l, 1], ln_b[l, 1])
        xl = _post_norm(xl, _half_ffn(xl, *ml[6:9], *ffn2), ln_g[l, 2], ln_b[l, 2])
        if need_ctx:
            xc = _post_norm(xc, mc[5] * yc, ln_g[l, 1], ln_b[l, 1])
            xc = _post_norm(xc, _half_ffn(xc, *mc[6:9], *ffn2), ln_g[l, 2], ln_b[l, 2])
    return xl
```

```python
import functools
import math

import jax
import jax.numpy as jnp
from jax import lax
from jax.experimental import pallas as pl
from jax.experimental.pallas import tpu as pltpu

f32 = jnp.float32
bf16 = jnp.bfloat16
HIGHEST = lax.Precision.HIGHEST

D_MODEL = 2048
DEPTH = 2
GRID_W = 64
D_FF = 5632
SHORT_CONV = 5
ROPE_THETA = 10000.0
N_MOD = 9
DN_HEADS, DN_DK, DN_DV, DN_CHUNK = 4, 128, 128, 64
GQA_HEADS, GQA_KV_HEADS, GQA_DIM = 4, 2, 128
SSD_HEADS, SSD_HEADDIM, SSD_GROUPS, SSD_STATE, SSD_CHUNK = 8, 64, 2, 128, 128
DIFF_HEADS, DIFF_DIM, DIFF_VDIM = 4, 64, 128
DN_W = DN_HEADS * DN_DV
GQA_W = GQA_HEADS * GQA_DIM
SSD_W = SSD_HEADS * SSD_HEADDIM
DIFF_W = DIFF_HEADS * DIFF_VDIM
D_MIX = DN_W + GQA_W + SSD_W + DIFF_W
DN_CONV_CH = 2 * DN_HEADS * DN_DK + DN_W
SSD_CONV_CH = SSD_W + 2 * SSD_GROUPS * SSD_STATE
ALPHA = (2 * DEPTH) ** 0.25

COL_DN_QKV = 0
COL_DN_Z = 1536
COL_GQA = 2048
COL_SSD_Z = 3072
COL_SSD_XBC = 3584
COL_DIFF = 4608
COL_SMALL = 6144
D_IN_PACKED = 6272
LANE_DN_B, LANE_DN_A, LANE_SSD_DT = 0, 8, 16

VMEM_LIMIT = 56 * 1024 * 1024
LANES = 128


def _cp(n_grid):
    return pltpu.CompilerParams(dimension_semantics=("arbitrary",) * n_grid, vmem_limit_bytes=VMEM_LIMIT)


def _dot(a, b):
    return jnp.dot(a, b, preferred_element_type=f32)


def _dot_nt(a, b):
    return lax.dot_general(a, b, (((1,), (1,)), ((), ())), preferred_element_type=f32)


def _dot_tn(a, b):
    return lax.dot_general(a, b, (((0,), (0,)), ((), ())), preferred_element_type=f32)


def _dot_exact(a, b):
    return jnp.dot(a, b, preferred_element_type=f32, precision=HIGHEST)


def _sigmoid(x):
    return 1.0 / (1.0 + jnp.exp(-x))


def _silu(x):
    return x * _sigmoid(x)


def _softplus(x):
    return jnp.maximum(x, 0.0) + jnp.log(1.0 + jnp.exp(-jnp.abs(x)))


def _layernorm(h, g, b):
    mu = jnp.mean(h, axis=-1, keepdims=True)
    hc = h - mu
    var = jnp.mean(hc * hc, axis=-1, keepdims=True)
    return hc * lax.rsqrt(var + 1e-5) * g + b


def _rms(x, g):
    return x * lax.rsqrt(jnp.mean(x * x, axis=-1, keepdims=True) + 1e-6) * g


def _rope(x, cos, sin_signed, blk):
    lane = lax.broadcasted_iota(jnp.int32, x.shape, 1)
    first = (lane & blk) == 0
    partner = jnp.where(first, pltpu.roll(x, LANES - blk, 1), pltpu.roll(x, blk, 1))
    return x * cos + partner * sin_signed


def _mod_body(s_ref, w_ref, b_ref, o_ref):
    o_ref[0] = _dot(_silu(s_ref[...]), w_ref[0]) + b_ref[0]


def _modulation(cvec, w_mod, b_mod):
    n_layers, d, n = w_mod.shape
    rows = cvec.shape[0]
    tn = 1024
    return pl.pallas_call(
        _mod_body,
        grid=(n_layers, n // tn),
        in_specs=[pl.BlockSpec((rows, d), lambda l, j: (0, 0)),
                  pl.BlockSpec((1, d, tn), lambda l, j: (l, 0, j)),
                  pl.BlockSpec((1, 1, tn), lambda l, j: (l, 0, j))],
        out_specs=pl.BlockSpec((1, rows, tn), lambda l, j: (l, 0, j)),
        out_shape=jax.ShapeDtypeStruct((n_layers, rows, n), f32),
        compiler_params=_cp(2), name="modulation",
    )(cvec, w_mod, b_mod.reshape(n_layers, 1, n))


def _ffn_body(x_ref, mod_ref, wg_ref, wu_ref, wd_ref, g_ref, b_ref, o_ref, hm_ref, acc_ref, *, base, nf):
    f = pl.program_id(1)

    @pl.when(f == 0)
    def _():
        hm_ref[...] = (x_ref[...] * (1.0 + mod_ref[base + 1:base + 2, :]) + mod_ref[base:base + 1, :]).astype(bf16)
        acc_ref[...] = jnp.zeros_like(acc_ref)

    hm = hm_ref[...]
    g = _dot(hm, wg_ref[...])
    u = _dot(hm, wu_ref[...])
    acc_ref[...] += _dot((_silu(g) * u).astype(bf16), wd_ref[...])

    @pl.when(f == nf - 1)
    def _():
        h = ALPHA * x_ref[...] + (0.5 * mod_ref[base + 2:base + 3, :]) * acc_ref[...]
        o_ref[...] = _layernorm(h, g_ref[...], b_ref[...])


def _row_tile(rows_per_mod):
    tm = min(512, rows_per_mod)
    assert rows_per_mod % tm == 0
    return tm


def _ffn(x, mods, mod_row0, rows_per_mod, base, wg, wu, wd, l, j, ln_g, ln_b, k):
    m, d = x.shape
    tm = _row_tile(rows_per_mod)
    tf = 512
    nf = D_FF // tf
    tpm = rows_per_mod // tm
    return pl.pallas_call(
        functools.partial(_ffn_body, base=base, nf=nf),
        grid=(m // tm, nf),
        in_specs=[pl.BlockSpec((tm, d), lambda i, f: (i, 0)),
                  pl.BlockSpec((None, None, N_MOD, d), lambda i, f: (l, mod_row0 + i // tpm, 0, 0)),
                  pl.BlockSpec((None, None, d, tf), lambda i, f: (l, j, 0, f)),
                  pl.BlockSpec((None, None, d, tf), lambda i, f: (l, j, 0, f)),
                  pl.BlockSpec((None, None, tf, d), lambda i, f: (l, j, f, 0)),
                  pl.BlockSpec((None, None, 1, d), lambda i, f: (l, k, 0, 0)),
                  pl.BlockSpec((None, None, 1, d), lambda i, f: (l, k, 0, 0))],
        out_specs=pl.BlockSpec((tm, d), lambda i, f: (i, 0)),
        out_shape=jax.ShapeDtypeStruct((m, d), f32),
        scratch_shapes=[pltpu.VMEM((tm, d), bf16), pltpu.VMEM((tm, d), f32)],
        compiler_params=_cp(2), name="half_ffn",
    )(x, mods, wg, wu, wd, ln_g, ln_b)


def _inproj_body(x_ref, mod_ref, w_ref, o_ref, hm_ref):
    @pl.when(pl.program_id(1) == 0)
    def _():
        hm_ref[...] = (x_ref[...] * (1.0 + mod_ref[4:5, :]) + mod_ref[3:4, :]).astype(bf16)

    o_ref[...] = _dot(hm_ref[...], w_ref[...])


def _inproj(x, mods, mod_row0, rows_per_mod, w, l):
    m, d = x.shape
    tm = _row_tile(rows_per_mod)
    tn = 896
    tpm = rows_per_mod // tm
    return pl.pallas_call(
        _inproj_body,
        grid=(m // tm, D_IN_PACKED // tn),
        in_specs=[pl.BlockSpec((tm, d), lambda i, n: (i, 0)),
                  pl.BlockSpec((None, None, N_MOD, d), lambda i, n: (l, mod_row0 + i // tpm, 0, 0)),
                  pl.BlockSpec((None, d, tn), lambda i, n: (l, 0, n))],
        out_specs=pl.BlockSpec((tm, tn), lambda i, n: (i, n)),
        out_shape=jax.ShapeDtypeStruct((m, D_IN_PACKED), f32),
        scratch_shapes=[pltpu.VMEM((tm, d), bf16)],
        compiler_params=_cp(2), name="in_proj",
    )(x, mods, w)


def _outproj_body(x_ref, a_ref, b_ref, c_ref, d_ref, w_ref, mod_ref, g_ref, be_ref, o_ref):
    y = _dot(a_ref[...], w_ref[0:DN_W, :])
    y += _dot(b_ref[...], w_ref[DN_W:DN_W + GQA_W, :])
    y += _dot(c_ref[...], w_ref[DN_W + GQA_W:DN_W + GQA_W + SSD_W, :])
    y += _dot(d_ref[...], w_ref[DN_W + GQA_W + SSD_W:D_MIX, :])
    h = ALPHA * x_ref[...] + mod_ref[5:6, :] * y
    o_ref[...] = _layernorm(h, g_ref[...], be_ref[...])


def _outproj(x, outs, w, mods, mod_row0, rows_per_mod, l, ln_g, ln_b):
    m, d = x.shape
    tm = _row_tile(rows_per_mod)
    tpm = rows_per_mod // tm
    mix_specs = [pl.BlockSpec((tm, o.shape[1]), lambda i: (i, 0)) for o in outs]
    return pl.pallas_call(
        _outproj_body,
        grid=(m // tm,),
        in_specs=[pl.BlockSpec((tm, d), lambda i: (i, 0))] + mix_specs + [
            pl.BlockSpec((None, D_MIX, d), lambda i: (l, 0, 0)),
            pl.BlockSpec((None, None, N_MOD, d), lambda i: (l, mod_row0 + i // tpm, 0, 0)),
            pl.BlockSpec((None, None, 1, d), lambda i: (l, 1, 0, 0)),
            pl.BlockSpec((None, None, 1, d), lambda i: (l, 1, 0, 0))],
        out_specs=pl.BlockSpec((tm, d), lambda i: (i, 0)),
        out_shape=jax.ShapeDtypeStruct((m, d), f32),
        compiler_params=_cp(1), name="out_proj",
    )(x, *outs, w, mods, ln_g, ln_b)


CONV_ROWS = 128
CONV_PAD = 8


def _conv_body(x_ref, w_ref, b_ref, o_ref, pad_ref, *, t, n_q, n_l2):
    cb = pl.program_id(1)
    c = x_ref.shape[1]
    pad_ref[0:CONV_PAD, :] = jnp.zeros((CONV_PAD, c), f32)
    pad_ref[CONV_PAD + t:2 * CONV_PAD + t, :] = jnp.zeros((CONV_PAD, c), f32)
    pad_ref[CONV_PAD:CONV_PAD + t, :] = x_ref[...]
    half = SHORT_CONV // 2
    rows = min(CONV_ROWS, t)
    for c0 in range(0, c, LANES):
        w = w_ref[:, c0:c0 + LANES]
        bias = b_ref[:, c0:c0 + LANES]
        for r0 in range(0, t, rows):
            acc = bias + pad_ref[pl.ds(CONV_PAD - half + r0, rows), c0:c0 + LANES] * w[0:1, :]
            for j in range(1, SHORT_CONV):
                acc += pad_ref[pl.ds(CONV_PAD - half + r0 + j, rows), c0:c0 + LANES] * w[j:j + 1, :]
            y = _silu(acc)
            if n_l2:
                r = lax.rsqrt(jnp.sum(y * y, axis=-1, keepdims=True) + 1e-6)
                y = y * jnp.where(cb < n_q, r * DN_DK ** -0.5, jnp.where(cb < n_l2, r, 1.0))
            o_ref[r0:r0 + rows, c0:c0 + LANES] = y.astype(o_ref.dtype)


def _conv(p, t, col0, n_ch, w, b, n_q, n_l2):
    m = p.shape[0]
    cw = 512
    cb0 = col0 // cw
    return pl.pallas_call(
        functools.partial(_conv_body, t=t, n_q=n_q, n_l2=n_l2),
        grid=(m // t, n_ch // cw),
        in_specs=[pl.BlockSpec((t, cw), lambda i, j: (i, cb0 + j)),
                  pl.BlockSpec((SHORT_CONV, cw), lambda i, j: (0, j)),
                  pl.BlockSpec((1, cw), lambda i, j: (0, j))],
        out_specs=pl.BlockSpec((t, cw), lambda i, j: (i, j)),
        out_shape=jax.ShapeDtypeStruct((m, n_ch), bf16),
        scratch_shapes=[pltpu.VMEM((t + 2 * CONV_PAD, cw), f32)],
        compiler_params=_cp(2), name="short_conv",
    )(p, w, b)


def _gqa_body(*refs, tc, tl, rope_q):
    if tl:
        (q_ref, kc_ref, vc_ref, kl_ref, vl_ref, cosq_ref, sinq_ref, cosk_ref, sink_ref,
         qn_ref, kn_ref, o_ref, k_s, v_s) = refs
    else:
        q_ref, kc_ref, vc_ref, qn_ref, kn_ref, o_ref, k_s, v_s = refs

    @pl.when(pl.program_id(2) == 0)
    def _():
        k_s[0:tc, :] = _rms(kc_ref[...], kn_ref[...]).astype(bf16)
        v_s[0:tc, :] = vc_ref[...].astype(bf16)
        if tl:
            rb = min(512, tl)
            for r0 in range(0, tl, rb):
                kk = _rms(kl_ref[r0:r0 + rb, :], kn_ref[...])
                kk = _rope(kk, cosk_ref[r0:r0 + rb, :], sink_ref[r0:r0 + rb, :], GQA_DIM // 4)
                k_s[tc + r0:tc + r0 + rb, :] = kk.astype(bf16)
                v_s[tc + r0:tc + r0 + rb, :] = vl_ref[r0:r0 + rb, :].astype(bf16)

    kall = k_s[...]
    vall = v_s[...]
    for r in range(GQA_HEADS // GQA_KV_HEADS):
        q = _rms(q_ref[:, r * GQA_DIM:(r + 1) * GQA_DIM], qn_ref[...])
        if rope_q:
            q = _rope(q, cosq_ref[...], sinq_ref[...], GQA_DIM // 4)
        s = _dot_nt((q * GQA_DIM ** -0.5).astype(bf16), kall)
        p = jnp.exp(s - jnp.max(s, axis=-1, keepdims=True))
        o = _dot(p.astype(bf16), vall) * (1.0 / jnp.sum(p, axis=-1, keepdims=True))
        o_ref[:, r * GQA_DIM:(r + 1) * GQA_DIM] = o.astype(o_ref.dtype)


def _gqa(pq, tq_seq, pc, tc, pl_, tl, rope, qn, kn):
    m = pq.shape[0]
    nb = m // tq_seq
    tq = min(256, tq_seq)
    rep = GQA_HEADS // GQA_KV_HEADS
    qw = rep * GQA_DIM
    qb0 = COL_GQA // qw
    kb0 = (COL_GQA + GQA_HEADS * GQA_DIM) // GQA_DIM
    vb0 = kb0 + GQA_KV_HEADS
    kv_spec = lambda t, b0: pl.BlockSpec((t, GQA_DIM), lambda b, g, i: (b, b0 + g))
    vec = pl.BlockSpec((1, GQA_DIM), lambda b, g, i: (0, 0))
    nq = tq_seq // tq
    in_specs = [pl.BlockSpec((tq, qw), lambda b, g, i: (b * nq + i, qb0 + g)), kv_spec(tc, kb0), kv_spec(tc, vb0)]
    args = [pq, pc, pc]
    if tl:
        cos, sin = rope
        in_specs += [kv_spec(tl, kb0), kv_spec(tl, vb0),
                     pl.BlockSpec((tq, GQA_DIM), lambda b, g, i: (i, 0)), pl.BlockSpec((tq, GQA_DIM), lambda b, g, i: (i, 0)),
                     pl.BlockSpec((tl, GQA_DIM), lambda b, g, i: (0, 0)), pl.BlockSpec((tl, GQA_DIM), lambda b, g, i: (0, 0))]
        args += [pl_, pl_, cos, sin, cos, sin]
    in_specs += [vec, vec]
    args += [qn, kn]
    return pl.pallas_call(
        functools.partial(_gqa_body, tc=tc, tl=tl, rope_q=bool(tl)),
        grid=(nb, GQA_KV_HEADS, nq),
        in_specs=in_specs,
        out_specs=pl.BlockSpec((tq, qw), lambda b, g, i: (b * nq + i, g)),
        out_shape=jax.ShapeDtypeStruct((m, GQA_W), bf16),
        scratch_shapes=[pltpu.VMEM((tc + tl, GQA_DIM), bf16), pltpu.VMEM((tc + tl, GQA_DIM), bf16)],
        compiler_params=_cp(3), name="gqa_attention",
    )(*args)


def _diff_body(*refs, tc, tl, lam_init):
    if tl:
        (q_ref, kc_ref, vc_ref, kl_ref, vl_ref, cosq_ref, sinq_ref, cosk_ref, sink_ref,
         lam_ref, ng_ref, o_ref, k_s, v_s) = refs
    else:
        q_ref, kc_ref, vc_ref, lam_ref, ng_ref, o_ref, k_s, v_s = refs

    @pl.when(pl.program_id(2) == 0)
    def _():
        k_s[0:tc, :] = kc_ref[...].astype(bf16)
        v_s[0:tc, :] = vc_ref[...].astype(bf16)
        if tl:
            rb = min(512, tl)
            for r0 in range(0, tl, rb):
                kk = _rope(kl_ref[r0:r0 + rb, :], cosk_ref[r0:r0 + rb, :], sink_ref[r0:r0 + rb, :], DIFF_DIM // 4)
                k_s[tc + r0:tc + r0 + rb, :] = kk.astype(bf16)
                v_s[tc + r0:tc + r0 + rb, :] = vl_ref[r0:r0 + rb, :].astype(bf16)

    lv = lam_ref[...]
    lam = (jnp.exp(jnp.sum(lv[0:1, :] * lv[1:2, :], axis=-1, keepdims=True))
           - jnp.exp(jnp.sum(lv[2:3, :] * lv[3:4, :], axis=-1, keepdims=True)) + lam_init)
    q = q_ref[...]
    if tl:
        q = _rope(q, cosq_ref[...], sinq_ref[...], DIFF_DIM // 4)
    q = q * DIFF_DIM ** -0.5
    lane = lax.broadcasted_iota(jnp.int32, q.shape, 1)
    kall = k_s[...]
    probs = []
    for i in range(2):
        qi = jnp.where((lane < DIFF_DIM) if i == 0 else (lane >= DIFF_DIM), q, 0.0).astype(bf16)
        s = _dot_nt(qi, kall)
        p = jnp.exp(s - jnp.max(s, axis=-1, keepdims=True))
        probs.append(p * (1.0 / jnp.sum(p, axis=-1, keepdims=True)))
    a = probs[0] - lam * probs[1]
    o = _dot(a.astype(bf16), v_s[...])
    o_ref[...] = (_rms(o, ng_ref[...]) * (1.0 - lam_init)).astype(o_ref.dtype)


def _diff(pq, tq_seq, pc, tc, pl_, tl, rope, lam_vecs, ng, lam_init):
    m = pq.shape[0]
    nb = m // tq_seq
    tq = min(256, tq_seq)
    nq = tq_seq // tq
    w = DIFF_VDIM
    qb0 = COL_DIFF // w
    kb0 = qb0 + DIFF_HEADS
    vb0 = kb0 + DIFF_HEADS
    kv_spec = lambda t, b0: pl.BlockSpec((t, w), lambda b, h, i: (b, b0 + h))
    in_specs = [pl.BlockSpec((tq, w), lambda b, h, i: (b * nq + i, qb0 + h)), kv_spec(tc, kb0), kv_spec(tc, vb0)]
    args = [pq, pc, pc]
    if tl:
        cos, sin = rope
        in_specs += [kv_spec(tl, kb0), kv_spec(tl, vb0),
                     pl.BlockSpec((tq, w), lambda b, h, i: (i, 0)), pl.BlockSpec((tq, w), lambda b, h, i: (i, 0)),
                     pl.BlockSpec((tl, w), lambda b, h, i: (0, 0)), pl.BlockSpec((tl, w), lambda b, h, i: (0, 0))]
        args += [pl_, pl_, cos, sin, cos, sin]
    in_specs += [pl.BlockSpec((4, DIFF_DIM), lambda b, h, i: (0, 0)), pl.BlockSpec((1, w), lambda b, h, i: (0, 0))]
    args += [lam_vecs, ng]
    return pl.pallas_call(
        functools.partial(_diff_body, tc=tc, tl=tl, lam_init=lam_init),
        grid=(nb, DIFF_HEADS, nq),
        in_specs=in_specs,
        out_specs=pl.BlockSpec((tq, w), lambda b, h, i: (b * nq + i, h)),
        out_shape=jax.ShapeDtypeStruct((m, DIFF_W), bf16),
        scratch_shapes=[pltpu.VMEM((tc + tl, w), bf16), pltpu.VMEM((tc + tl, w), bf16)],
        compiler_params=_cp(3), name="diff_attention",
    )(*args)


def _tri_masks(n):
    row = lax.broadcasted_iota(jnp.int32, (n, n), 0)
    col = lax.broadcasted_iota(jnp.int32, (n, n), 1)
    return row >= col, row <= col, row > col, row < col


def _dn_body(*refs, tc, tl, need_ctx):
    (qc_ref, kc_ref, vc_ref, zc_ref, gc_ref, gtc_ref,
     ql_ref, kl_ref, vl_ref, zl_ref, gl_ref, gtl_ref, cp_ref, rp_ref, ng_ref) = refs[:15]
    if need_ctx:
        oc_ref, ol_ref, o_s, s_s = refs[15:]
    else:
        ol_ref, o_s, s_s = refs[15:]
        oc_ref = None
    c = DN_CHUNK
    ge, le, gt, lt = _tri_masks(c)
    incl = (ge, le)
    strict = (gt, lt)
    tri_col = (ge.astype(f32), le.astype(f32))
    tri_row = (le.astype(f32), ge.astype(f32))
    eye = (lax.broadcasted_iota(jnp.int32, (c, c), 0) == lax.broadcasted_iota(jnp.int32, (c, c), 1)).astype(f32)
    neg_a_row = -jnp.exp(cp_ref[0:1, :])
    dtb_row = cp_ref[1:2, :]
    neg_a_col = -jnp.exp(rp_ref[:, 0:1])
    dtb_col = rp_ref[:, 1:2]
    s_s[...] = jnp.zeros_like(s_s)

    def run(q_ref, k_ref, v_ref, g_ref, gt_ref, n_chunks, base):
        def step(i, carry):
            for d in range(2):
                ci = i if d == 0 else n_chunks - 1 - i
                r = pl.multiple_of(ci * c, c)
                graw = g_ref[pl.ds(r, c), :]
                beta_all = _sigmoid(graw)
                gdec = neg_a_row * _softplus(graw + dtb_row)
                cum = _dot_exact(tri_col[d], gdec)
                total = jnp.sum(gdec, axis=0, keepdims=True)
                gdec_r = neg_a_col * _softplus(gt_ref[ci] + dtb_col)
                cum_r = _dot_exact(gdec_r, tri_row[d])
                for h in range(DN_HEADS):
                    lb = LANE_DN_B + d * DN_HEADS + h
                    la = LANE_DN_A + d * DN_HEADS + h
                    beta = beta_all[:, lb:lb + 1]
                    gcol = cum[:, la:la + 1]
                    grow = cum_r[la:la + 1, :]
                    tot = total[:, la:la + 1]
                    decay = jnp.where(incl[d], jnp.exp(jnp.where(incl[d], gcol - grow, 0.0)), 0.0)
                    hs = slice(h * DN_DK, (h + 1) * DN_DK)
                    q16 = q_ref[pl.ds(r, c), hs]
                    k16 = k_ref[pl.ds(r, c), hs]
                    kf = k16.astype(f32)
                    vf = v_ref[pl.ds(r, c), hs].astype(f32)
                    kk = _dot_nt(k16, k16)
                    qk = _dot_nt(q16, k16) * decay
                    m = -jnp.where(strict[d], beta * kk * decay, 0.0)
                    inv = eye + m
                    for _ in range(int(math.log2(c)) - 1):
                        m16 = m.astype(bf16)
                        m = _dot(m16, m16)
                        inv = _dot(inv.astype(bf16), (eye + m).astype(bf16))
                    inv16 = inv.astype(bf16)
                    eg = jnp.exp(gcol)
                    u = _dot(inv16, (vf * beta).astype(bf16))
                    w = _dot(inv16, (kf * (beta * eg)).astype(bf16))
                    si = d * DN_HEADS + h
                    s = s_s[si]
                    s16 = s.astype(bf16)
                    v_new = u - _dot(w.astype(bf16), s16)
                    v16 = v_new.astype(bf16)
                    o = _dot((q16.astype(f32) * eg).astype(bf16), s16) + _dot(qk.astype(bf16), v16)
                    s_s[si] = s * jnp.exp(tot) + _dot_tn((kf * jnp.exp(tot - gcol)).astype(bf16), v16)
                    o_s[d, pl.ds(base + r, c), hs] = o
            return carry

        lax.fori_loop(0, n_chunks, step, 0)

    run(qc_ref, kc_ref, vc_ref, gc_ref, gtc_ref, tc // c, 0)
    run(ql_ref, kl_ref, vl_ref, gl_ref, gtl_ref, tl // c, tc)

    def finish(z_ref, out_ref, base, t):
        rb = min(256, t)
        for r0 in range(0, t, rb):
            for h in range(DN_HEADS):
                hs = slice(h * DN_DV, (h + 1) * DN_DV)
                o = o_s[0, base + r0:base + r0 + rb, hs] + o_s[1, base + r0:base + r0 + rb, hs]
                out_ref[r0:r0 + rb, hs] = (_rms(o, ng_ref[...]) * _silu(z_ref[r0:r0 + rb, hs])).astype(out_ref.dtype)

    finish(zl_ref, ol_ref, tc, tl)
    if need_ctx:
        finish(zc_ref, oc_ref, 0, tc)


def _deltanet(qkv_c, p_c, gt_c, tc, qkv_l, p_l, gt_l, tl, cparams, rparams, ng, need_ctx):
    nb = qkv_l.shape[0] // tl
    w = DN_W
    c = DN_CHUNK

    def seq_specs(t):
        return [pl.BlockSpec((t, w), lambda b: (b, 0)), pl.BlockSpec((t, w), lambda b: (b, 1)),
                pl.BlockSpec((t, w), lambda b: (b, 2)),
                pl.BlockSpec((t, w), lambda b: (b, COL_DN_Z // w)),
                pl.BlockSpec((t, LANES), lambda b: (b, COL_SMALL // LANES)),
                pl.BlockSpec((None, t // c, 32, c), lambda b: (b, 0, 0, 0))]

    in_specs = seq_specs(tc) + seq_specs(tl) + [
        pl.BlockSpec((8, LANES), lambda b: (0, 0)), pl.BlockSpec((32, 2), lambda b: (0, 0)),
        pl.BlockSpec((1, DN_DV), lambda b: (0, 0))]
    out_l = jax.ShapeDtypeStruct((nb * tl, w), bf16)
    out_c = jax.ShapeDtypeStruct((nb * tc, w), bf16)
    spec_l = pl.BlockSpec((tl, w), lambda b: (b, 0))
    spec_c = pl.BlockSpec((tc, w), lambda b: (b, 0))
    res = pl.pallas_call(
        functools.partial(_dn_body, tc=tc, tl=tl, need_ctx=need_ctx),
        grid=(nb,),
        in_specs=in_specs,
        out_specs=(spec_c, spec_l) if need_ctx else spec_l,
        out_shape=(out_c, out_l) if need_ctx else out_l,
        scratch_shapes=[pltpu.VMEM((2, tc + tl, w), f32), pltpu.VMEM((2 * DN_HEADS, DN_DK, DN_DV), f32)],
        compiler_params=_cp(1), name="gated_deltanet",
    )(qkv_c, qkv_c, qkv_c, p_c, p_c, gt_c, qkv_l, qkv_l, qkv_l, p_l, p_l, gt_l, cparams, rparams, ng)
    return (res[1], res[0]) if need_ctx else (res, None)


def _ssd_body(*refs, tc, tl, need_ctx):
    (xc_ref, zc_ref, gc_ref, gtc_ref, xl_ref, zl_ref, gl_ref, gtl_ref,
     cp_ref, rp_ref, ex_ref, dsk_ref, ng_ref) = refs[:13]
    if need_ctx:
        oc_ref, ol_ref, y_s, s_s = refs[13:]
    else:
        ol_ref, y_s, s_s = refs[13:]
        oc_ref = None
    c = SSD_CHUNK
    ge, le, _, _ = _tri_masks(c)
    incl = (ge, le)
    tri_col = (ge.astype(f32), le.astype(f32))
    tri_row = (le.astype(f32), ge.astype(f32))
    neg_a_row = -jnp.exp(cp_ref[0:1, :])
    dtb_row = cp_ref[1:2, :]
    neg_a_col = -jnp.exp(rp_ref[:, 0:1])
    dtb_col = rp_ref[:, 1:2]
    lane = lax.broadcasted_iota(jnp.int32, (c, LANES), 1)
    lo = lane < SSD_HEADDIM
    hi = lane >= SSD_HEADDIM
    s_s[...] = jnp.zeros_like(s_s)
    hpg = SSD_HEADS // SSD_GROUPS
    gw = hpg * SSD_HEADDIM
    b0 = SSD_W
    c0 = SSD_W + SSD_GROUPS * SSD_STATE

    def run(x_ref, g_ref, gt_ref, n_chunks, base):
        def step(i, carry):
            for d in range(2):
                ci = i if d == 0 else n_chunks - 1 - i
                r = pl.multiple_of(ci * c, c)
                graw = g_ref[pl.ds(r, c), :]
                dt = _softplus(graw + dtb_row)
                da = dt * neg_a_row
                cum = _dot_exact(tri_col[d], da)
                total = jnp.sum(da, axis=0, keepdims=True)
                da_r = _softplus(gt_ref[ci] + dtb_col) * neg_a_col
                cum_r = _dot_exact(da_r, tri_row[d])
                ex = ex_ref[d]
                dt_e = _dot_exact(dt, ex)
                dtw_e = _dot_exact(dt * jnp.exp(total - cum), ex)
                cin_e = _dot_exact(jnp.exp(cum), ex)
                cd_e = _dot_exact(jnp.broadcast_to(jnp.exp(total), (8, LANES)), ex)[0:1, :]
                xs = x_ref[pl.ds(r, c), 0:SSD_W].astype(f32)
                xdt = xs * dt_e
                xdtw = (xs * dtw_e).astype(bf16)
                for g in range(SSD_GROUPS):
                    bm = x_ref[pl.ds(r, c), b0 + g * SSD_STATE:b0 + (g + 1) * SSD_STATE]
                    cm = x_ref[pl.ds(r, c), c0 + g * SSD_STATE:c0 + (g + 1) * SSD_STATE]
                    scores = _dot_nt(cm, bm)
                    gs = slice(g * gw, (g + 1) * gw)
                    si = d * SSD_GROUPS + g
                    s = s_s[si]
                    y_off = _dot(cm, s.astype(bf16)) * cin_e[:, gs]
                    s_s[si] = s * cd_e[:, gs] + _dot_tn(bm, xdtw[:, gs])
                    for pair in range(hpg // 2):
                        ps = slice(g * gw + pair * LANES, g * gw + (pair + 1) * LANES)
                        xp = xdt[:, ps]
                        yp = y_off[:, pair * LANES:(pair + 1) * LANES]
                        for e in range(2):
                            h = g * hpg + pair * 2 + e
                            la = LANE_SSD_DT + d * SSD_HEADS + h
                            seg = cum[:, la:la + 1] - cum_r[la:la + 1, :]
                            decay = jnp.where(incl[d], jnp.exp(jnp.where(incl[d], seg, 0.0)), 0.0)
                            xh = jnp.where(lo if e == 0 else hi, xp, 0.0).astype(bf16)
                            yp = yp + _dot((scores * decay).astype(bf16), xh)
                        y_s[d, pl.ds(base + r, c), ps] = yp
            return carry

        lax.fori_loop(0, n_chunks, step, 0)

    run(xc_ref, gc_ref, gtc_ref, tc // c, 0)
    run(xl_ref, gl_ref, gtl_ref, tl // c, tc)

    def finish(x_ref, z_ref, out_ref, base, t):
        rb = min(256, t)
        gcols = SSD_W // SSD_GROUPS
        for r0 in range(0, t, rb):
            for g in range(SSD_GROUPS):
                gs = slice(g * gcols, (g + 1) * gcols)
                y = y_s[0, base + r0:base + r0 + rb, gs] + y_s[1, base + r0:base + r0 + rb, gs]
                y = (y + dsk_ref[:, gs] * x_ref[r0:r0 + rb, gs].astype(f32)) * _silu(z_ref[r0:r0 + rb, gs])
                out_ref[r0:r0 + rb, gs] = _rms(y, ng_ref[:, gs]).astype(out_ref.dtype)

    finish(xl_ref, zl_ref, ol_ref, tc, tl)
    if need_ctx:
        finish(xc_ref, zc_ref, oc_ref, 0, tc)


def _ssd(xbc_c, p_c, gt_c, tc, xbc_l, p_l, gt_l, tl, cparams, rparams, expand, d_skip, ng, need_ctx):
    nb = xbc_l.shape[0] // tl
    w = SSD_W
    c = SSD_CHUNK

    def seq_specs(t):
        return [pl.BlockSpec((t, SSD_CONV_CH), lambda b: (b, 0)),
                pl.BlockSpec((t, w), lambda b: (b, COL_SSD_Z // w)),
                pl.BlockSpec((t, LANES), lambda b: (b, COL_SMALL // LANES)),
                pl.BlockSpec((None, t // c, 32, c), lambda b: (b, 0, 0, 0))]

    in_specs = seq_specs(tc) + seq_specs(tl) + [
        pl.BlockSpec((8, LANES), lambda b: (0, 0)), pl.BlockSpec((32, 2), lambda b: (0, 0)),
        pl.BlockSpec((2, LANES, w), lambda b: (0, 0, 0)),
        pl.BlockSpec((1, w), lambda b: (0, 0)), pl.BlockSpec((1, w), lambda b: (0, 0))]
    out_l = jax.ShapeDtypeStruct((nb * tl, w), bf16)
    out_c = jax.ShapeDtypeStruct((nb * tc, w), bf16)
    spec_l = pl.BlockSpec((tl, w), lambda b: (b, 0))
    spec_c = pl.BlockSpec((tc, w), lambda b: (b, 0))
    res = pl.pallas_call(
        functools.partial(_ssd_body, tc=tc, tl=tl, need_ctx=need_ctx),
        grid=(nb,),
        in_specs=in_specs,
        out_specs=(spec_c, spec_l) if need_ctx else spec_l,
        out_shape=(out_c, out_l) if need_ctx else out_l,
        scratch_shapes=[pltpu.VMEM((2, tc + tl, w), f32),
                        pltpu.VMEM((2 * SSD_GROUPS, SSD_STATE, w // SSD_GROUPS), f32)],
        compiler_params=_cp(1), name="ssd_scan",
    )(xbc_c, p_c, p_c, gt_c, xbc_l, p_l, p_l, gt_l, cparams, rparams, expand, d_skip, ng)
    return (res[1], res[0]) if need_ctx else (res, None)


def _rope_tables(n_tok, head_dim, reps):
    rows = n_tok // GRID_W
    row = jnp.broadcast_to(jnp.arange(rows)[:, None], (rows, GRID_W)).reshape(-1).astype(f32)
    col = jnp.broadcast_to(jnp.arange(GRID_W)[None, :], (rows, GRID_W)).reshape(-1).astype(f32)
    half = head_dim // 2
    inv = ROPE_THETA ** (-jnp.arange(0, half, 2, dtype=f32) / half)
    ar = row[:, None] * inv
    ac = col[:, None] * inv
    cos = jnp.concatenate([jnp.cos(ar), jnp.cos(ar), jnp.cos(ac), jnp.cos(ac)], axis=-1)
    sin = jnp.concatenate([-jnp.sin(ar), jnp.sin(ar), -jnp.sin(ac), jnp.sin(ac)], axis=-1)
    return jnp.tile(cos, (1, reps)), jnp.tile(sin, (1, reps))


def _pack_w_in(w_in):
    a0 = 0
    a_small = DN_CONV_CH + DN_W
    b0 = a_small + 4 * DN_HEADS
    c0 = b0 + (GQA_HEADS + 2 * GQA_KV_HEADS) * GQA_DIM
    c_small = c0 + SSD_W + SSD_CONV_CH
    d0 = c_small + 2 * SSD_HEADS
    d1 = d0 + 4 * DIFF_HEADS * DIFF_DIM + DIFF_W
    pad = jnp.zeros(w_in.shape[:2] + (D_IN_PACKED - COL_SMALL - 32,), w_in.dtype)
    return jnp.concatenate([w_in[..., a0:a_small], w_in[..., b0:c0], w_in[..., c0:c_small], w_in[..., d0:d1],
                            w_in[..., a_small:b0], w_in[..., c_small:d0], pad], axis=-1)


def _gates_rowlayout(p, t, chunk):
    g = p[:, COL_SMALL:COL_SMALL + 32]
    nb = g.shape[0] // t
    return g.reshape(nb, t // chunk, chunk, 32).transpose(0, 1, 3, 2)


def _lane_params(a_log, dt_bias, lane0):
    n = a_log.size
    out = jnp.zeros((8, LANES), f32)
    out = out.at[0, lane0:lane0 + n].set(a_log.reshape(-1))
    return out.at[1, lane0:lane0 + n].set(dt_bias.reshape(-1))


def _row_params(a_log, dt_bias, row0):
    n = a_log.size
    out = jnp.zeros((32, 2), f32)
    out = out.at[row0:row0 + n, 0].set(a_log.reshape(-1))
    return out.at[row0:row0 + n, 1].set(dt_bias.reshape(-1))


def _ssd_expand():
    lane = jnp.arange(LANES)[:, None]
    colh = (jnp.arange(SSD_W) // SSD_HEADDIM)[None, :]
    return jnp.stack([(lane == LANE_SSD_DT + d * SSD_HEADS + colh).astype(f32) for d in range(2)])


def kernel(x, c, ctx, c_ctx, w_mod, b_mod, ln_g, ln_b, ffn_wg, ffn_wu, ffn_wd, w_in, w_out, dn_conv, dn_a_log, dn_dt_bias, dn_norm_g, gqa_q_norm, gqa_k_norm, ssd_conv, ssd_conv_b, ssd_a_log, ssd_dt_bias, ssd_d, ssd_norm_g, diff_lambda, diff_norm_g):
    nb, tl, d = x.shape
    tc = ctx.shape[1]
    n_layers = w_mod.shape[0]
    xl = x.reshape(nb * tl, d)
    xc = ctx.reshape(nb * tc, d)

    rows = -(-(nb + 1) // 8) * 8
    cvec = jnp.zeros((rows, d), f32).at[:nb].set(c).at[nb].set(c_ctx)
    mods = _modulation(cvec, w_mod, b_mod).reshape(n_layers, rows, N_MOD, d)

    wg16, wu16, wd16 = ffn_wg.astype(bf16), ffn_wu.astype(bf16), ffn_wd.astype(bf16)
    win16 = _pack_w_in(w_in).astype(bf16)
    wout16 = w_out.astype(bf16)
    lng = ln_g.reshape(n_layers, 3, 1, d)
    lnb = ln_b.reshape(n_layers, 3, 1, d)
    rope_gqa = _rope_tables(tl, GQA_DIM, 1)
    rope_diff = _rope_tables(tl, DIFF_DIM, 2)
    ssd_ex = _ssd_expand()
    zero_bias_dn = jnp.zeros((1, DN_CONV_CH), f32)

    lat = dict(mod_row0=0, rows_per_mod=tl)
    cx = dict(mod_row0=nb, rows_per_mod=nb * tc)

    for l in range(n_layers):
        need_ctx = l < n_layers - 1
        lam_init = 0.8 - 0.6 * math.exp(-0.3 * l)
        ffn = lambda h, grp, base, j, k: _ffn(h, mods, grp["mod_row0"], grp["rows_per_mod"], base,
                                              wg16, wu16, wd16, l, j, lng, lnb, k)
        xl = ffn(xl, lat, 0, 0, 0)
        xc = ffn(xc, cx, 0, 0, 0)

        p_l = _inproj(xl, mods, 0, tl, win16, l)
        p_c = _inproj(xc, mods, nb, nb * tc, win16, l)

        dn_qkv_l = _conv(p_l, tl, COL_DN_QKV, DN_CONV_CH, dn_conv[l], zero_bias_dn, 1, 2)
        dn_qkv_c = _conv(p_c, tc, COL_DN_QKV, DN_CONV_CH, dn_conv[l], zero_bias_dn, 1, 2)
        a_l, a_c = _deltanet(dn_qkv_c, p_c, _gates_rowlayout(p_c, tc, DN_CHUNK), tc,
                             dn_qkv_l, p_l, _gates_rowlayout(p_l, tl, DN_CHUNK), tl,
                             _lane_params(dn_a_log[l], dn_dt_bias[l], LANE_DN_A),
                             _row_params(dn_a_log[l], dn_dt_bias[l], LANE_DN_A),
                             dn_norm_g[l].reshape(1, DN_DV), need_ctx)
        qn = gqa_q_norm[l].reshape(1, GQA_DIM)
        kn = gqa_k_norm[l].reshape(1, GQA_DIM)
        b_l = _gqa(p_l, tl, p_c, tc, p_l, tl, rope_gqa, qn, kn)
        b_c = _gqa(p_c, tc, p_c, tc, None, 0, None, qn, kn) if need_ctx else None
        ssd_xbc_l = _conv(p_l, tl, COL_SSD_XBC, SSD_CONV_CH, ssd_conv[l], ssd_conv_b[l].reshape(1, -1), 0, 0)
        ssd_xbc_c = _conv(p_c, tc, COL_SSD_XBC, SSD_CONV_CH, ssd_conv[l], ssd_conv_b[l].reshape(1, -1), 0, 0)
        c_l, c_c = _ssd(ssd_xbc_c, p_c, _gates_rowlayout(p_c, tc, SSD_CHUNK), tc,
                        ssd_xbc_l, p_l, _gates_rowlayout(p_l, tl, SSD_CHUNK), tl,
                        _lane_params(ssd_a_log[l], ssd_dt_bias[l], LANE_SSD_DT),
                        _row_params(ssd_a_log[l], ssd_dt_bias[l], LANE_SSD_DT),
                        ssd_ex, jnp.repeat(ssd_d[l], SSD_HEADDIM).reshape(1, SSD_W),
                        ssd_norm_g[l].reshape(1, SSD_W), need_ctx)
        ng = diff_norm_g[l].reshape(1, DIFF_VDIM)
        d_l = _diff(p_l, tl, p_c, tc, p_l, tl, rope_diff, diff_lambda[l], ng, lam_init)
        d_c = _diff(p_c, tc, p_c, tc, None, 0, None, diff_lambda[l], ng, lam_init) if need_ctx else None

        xl = _outproj(xl, (a_l, b_l, c_l, d_l), wout16, mods, 0, tl, l, lng, lnb)
        xl = ffn(xl, lat, 6, 1, 2)
        if need_ctx:
            xc = _outproj(xc, (a_c, b_c, c_c, d_c), wout16, mods, nb, nb * tc, l, lng, lnb)
            xc = ffn(xc, cx, 6, 1, 2)
    return xl.reshape(nb, tl, d)
```

```python
import functools
import math

import jax
import jax.numpy as jnp
from jax import lax
from jax.experimental import pallas as pl
from jax.experimental.pallas import tpu as pltpu

f32 = jnp.float32
bf16 = jnp.bfloat16

D_MODEL = 2048
DEPTH = 2
GRID_W = 64
D_FF = 5632
SHORT_CONV = 5
ROPE_THETA = 10000.0
N_MOD = 9
DN_HEADS, DN_DK, DN_DV, DN_CHUNK = 4, 128, 128, 64
GQA_HEADS, GQA_KV_HEADS, GQA_DIM = 4, 2, 128
SSD_HEADS, SSD_HEADDIM, SSD_GROUPS, SSD_STATE, SSD_CHUNK = 8, 64, 2, 128, 128
DIFF_HEADS, DIFF_DIM, DIFF_VDIM = 4, 64, 128
DN_W = DN_HEADS * DN_DV
GQA_W = GQA_HEADS * GQA_DIM
SSD_W = SSD_HEADS * SSD_HEADDIM
DIFF_W = DIFF_HEADS * DIFF_VDIM
D_MIX = DN_W + GQA_W + SSD_W + DIFF_W
DN_CONV_CH = 2 * DN_HEADS * DN_DK + DN_W
SSD_CONV_CH = SSD_W + 2 * SSD_GROUPS * SSD_STATE
ALPHA = (2 * DEPTH) ** 0.25

COL_DN_QKV = 0
COL_DN_Z = 1536
COL_GQA = 2048
COL_SSD_Z = 3072
COL_SSD_XBC = 3584
COL_DIFF = 4608
D_MAIN = 6144
LANE_DN_B, LANE_DN_A, LANE_SSD_DT = 0, 8, 16
LOG2E = 1.4426950408889634

VMEM_LIMIT = 56 * 1024 * 1024
LANES = 128


def _cp(n_grid):
    return pltpu.CompilerParams(dimension_semantics=("arbitrary",) * n_grid, vmem_limit_bytes=VMEM_LIMIT)


def _dot(a, b):
    return jnp.dot(a, b, preferred_element_type=f32)


def _dot_nt(a, b):
    return lax.dot_general(a, b, (((1,), (1,)), ((), ())), preferred_element_type=f32)


def _dot_tn(a, b):
    return lax.dot_general(a, b, (((0,), (0,)), ((), ())), preferred_element_type=f32)


def _split_bf16(x, parts):
    out = []
    for _ in range(parts - 1):
        hi = x.astype(bf16)
        out.append(hi)
        x = x - hi.astype(f32)
    out.append(x.astype(bf16))
    return out


def _dot_sel_l(sel16, x, parts):
    n = x.shape[1]
    y = _dot(sel16, jnp.concatenate(_split_bf16(x, parts), axis=1))
    return sum(y[:, k * n:(k + 1) * n] for k in range(parts))


def _dot_sel_r(x, sel16, parts):
    m = x.shape[0]
    y = _dot(jnp.concatenate(_split_bf16(x, parts), axis=0), sel16)
    return sum(y[k * m:(k + 1) * m, :] for k in range(parts))


def _sigmoid(x):
    return 1.0 / (1.0 + jnp.exp(-x))


def _silu(x):
    return x * _sigmoid(x)


def _softplus(x):
    return jnp.maximum(x, 0.0) + jnp.log(1.0 + jnp.exp(-jnp.abs(x)))


def _layernorm(h, g, b):
    mu = jnp.mean(h, axis=-1, keepdims=True)
    hc = h - mu
    var = jnp.mean(hc * hc, axis=-1, keepdims=True)
    return hc * lax.rsqrt(var + 1e-5) * g + b


def _rms(x, g):
    return x * lax.rsqrt(jnp.mean(x * x, axis=-1, keepdims=True) + 1e-6) * g


def _rope(x, cos, sin_signed, blk):
    lane = lax.broadcasted_iota(jnp.int32, x.shape, 1)
    first = (lane & blk) == 0
    partner = jnp.where(first, pltpu.roll(x, LANES - blk, 1), pltpu.roll(x, blk, 1))
    return x * cos + partner * sin_signed


def _mod_body(s_ref, w_ref, b_ref, o_ref):
    o_ref[0] = _dot(_silu(s_ref[...]), w_ref[0]) + b_ref[0]


def _modulation(cvec, w_mod, b_mod):
    n_layers, d, n = w_mod.shape
    rows = cvec.shape[0]
    tn = 1024
    return pl.pallas_call(
        _mod_body,
        grid=(n_layers, n // tn),
        in_specs=[pl.BlockSpec((rows, d), lambda l, j: (0, 0)),
                  pl.BlockSpec((1, d, tn), lambda l, j: (l, 0, j)),
                  pl.BlockSpec((1, 1, tn), lambda l, j: (l, 0, j))],
        out_specs=pl.BlockSpec((1, rows, tn), lambda l, j: (l, 0, j)),
        out_shape=jax.ShapeDtypeStruct((n_layers, rows, n), f32),
        compiler_params=_cp(2), name="modulation",
    )(cvec, w_mod, b_mod.reshape(n_layers, 1, n))


def _ffn_body(x_ref, mod_ref, wg_ref, wu_ref, wd_ref, g_ref, b_ref, o_ref, hm_ref, acc_ref, *, base, nf):
    f = pl.program_id(1)

    @pl.when(f == 0)
    def _():
        hm_ref[...] = (x_ref[...] * (1.0 + mod_ref[base + 1:base + 2, :]) + mod_ref[base:base + 1, :]).astype(bf16)
        acc_ref[...] = jnp.zeros_like(acc_ref)

    hm = hm_ref[...]
    g = _dot(hm, wg_ref[...])
    u = _dot(hm, wu_ref[...])
    acc_ref[...] += _dot((_silu(g) * u).astype(bf16), wd_ref[...])

    @pl.when(f == nf - 1)
    def _():
        h = ALPHA * x_ref[...] + (0.5 * mod_ref[base + 2:base + 3, :]) * acc_ref[...]
        o_ref[...] = _layernorm(h, g_ref[...], b_ref[...])


def _row_tile(rows_per_mod):
    tm = min(512, rows_per_mod)
    assert rows_per_mod % tm == 0
    return tm


def _ffn(x, mods, mod_row0, rows_per_mod, base, wg, wu, wd, l, j, ln_g, ln_b, k):
    m, d = x.shape
    tm = _row_tile(rows_per_mod)
    tf = 512
    nf = D_FF // tf
    tpm = rows_per_mod // tm
    return pl.pallas_call(
        functools.partial(_ffn_body, base=base, nf=nf),
        grid=(m // tm, nf),
        in_specs=[pl.BlockSpec((tm, d), lambda i, f: (i, 0)),
                  pl.BlockSpec((None, None, N_MOD, d), lambda i, f: (l, mod_row0 + i // tpm, 0, 0)),
                  pl.BlockSpec((None, None, d, tf), lambda i, f: (l, j, 0, f)),
                  pl.BlockSpec((None, None, d, tf), lambda i, f: (l, j, 0, f)),
                  pl.BlockSpec((None, None, tf, d), lambda i, f: (l, j, f, 0)),
                  pl.BlockSpec((None, None, 1, d), lambda i, f: (l, k, 0, 0)),
                  pl.BlockSpec((None, None, 1, d), lambda i, f: (l, k, 0, 0))],
        out_specs=pl.BlockSpec((tm, d), lambda i, f: (i, 0)),
        out_shape=jax.ShapeDtypeStruct((m, d), f32),
        scratch_shapes=[pltpu.VMEM((tm, d), bf16), pltpu.VMEM((tm, d), f32)],
        compiler_params=_cp(2), name="half_ffn",
    )(x, mods, wg, wu, wd, ln_g, ln_b)


def _inproj_body(x_ref, mod_ref, w_ref, ws_ref, o_ref, os_ref, hm_ref):
    @pl.when(pl.program_id(1) == 0)
    def _():
        hm_ref[...] = (x_ref[...] * (1.0 + mod_ref[4:5, :]) + mod_ref[3:4, :]).astype(bf16)
        os_ref[...] = _dot(hm_ref[...], ws_ref[...])

    o_ref[...] = _dot(hm_ref[...], w_ref[...]).astype(o_ref.dtype)


def _inproj(x, mods, mod_row0, rows_per_mod, w, w_small, l):
    m, d = x.shape
    tm = 1024 if rows_per_mod % 1024 == 0 else _row_tile(rows_per_mod)
    tn = 1024
    tpm = rows_per_mod // tm
    return pl.pallas_call(
        _inproj_body,
        grid=(m // tm, D_MAIN // tn),
        in_specs=[pl.BlockSpec((tm, d), lambda i, n: (i, 0)),
                  pl.BlockSpec((None, None, N_MOD, d), lambda i, n: (l, mod_row0 + i // tpm, 0, 0)),
                  pl.BlockSpec((None, d, tn), lambda i, n: (l, 0, n)),
                  pl.BlockSpec((None, d, LANES), lambda i, n: (l, 0, 0))],
        out_specs=(pl.BlockSpec((tm, tn), lambda i, n: (i, n)), pl.BlockSpec((tm, LANES), lambda i, n: (i, 0))),
        out_shape=(jax.ShapeDtypeStruct((m, D_MAIN), bf16), jax.ShapeDtypeStruct((m, LANES), f32)),
        scratch_shapes=[pltpu.VMEM((tm, d), bf16)],
        compiler_params=_cp(2), name="in_proj",
    )(x, mods, w, w_small)


def _outproj_body(x_ref, a_ref, b_ref, c_ref, d_ref, w_ref, mod_ref, g_ref, be_ref, o_ref):
    y = _dot(a_ref[...], w_ref[0:DN_W, :])
    y += _dot(b_ref[...], w_ref[DN_W:DN_W + GQA_W, :])
    y += _dot(c_ref[...], w_ref[DN_W + GQA_W:DN_W + GQA_W + SSD_W, :])
    y += _dot(d_ref[...], w_ref[DN_W + GQA_W + SSD_W:D_MIX, :])
    h = ALPHA * x_ref[...] + mod_ref[5:6, :] * y
    o_ref[...] = _layernorm(h, g_ref[...], be_ref[...])


def _outproj(x, outs, w, mods, mod_row0, rows_per_mod, l, ln_g, ln_b):
    m, d = x.shape
    tm = _row_tile(rows_per_mod)
    tpm = rows_per_mod // tm
    mix_specs = [pl.BlockSpec((tm, o.shape[1]), lambda i: (i, 0)) for o in outs]
    return pl.pallas_call(
        _outproj_body,
        grid=(m // tm,),
        in_specs=[pl.BlockSpec((tm, d), lambda i: (i, 0))] + mix_specs + [
            pl.BlockSpec((None, D_MIX, d), lambda i: (l, 0, 0)),
            pl.BlockSpec((None, None, N_MOD, d), lambda i: (l, mod_row0 + i // tpm, 0, 0)),
            pl.BlockSpec((None, None, 1, d), lambda i: (l, 1, 0, 0)),
            pl.BlockSpec((None, None, 1, d), lambda i: (l, 1, 0, 0))],
        out_specs=pl.BlockSpec((tm, d), lambda i: (i, 0)),
        out_shape=jax.ShapeDtypeStruct((m, d), f32),
        compiler_params=_cp(1), name="out_proj",
    )(x, *outs, w, mods, ln_g, ln_b)


CONV_ROWS = 128
CONV_PAD = 8


def _conv_body(x_ref, w_ref, b_ref, o_ref, pad_ref, *, t, n_q, n_l2):
    cb = pl.program_id(1)
    c = x_ref.shape[1]
    pad_ref[0:CONV_PAD, :] = jnp.zeros((CONV_PAD, c), f32)
    pad_ref[CONV_PAD + t:2 * CONV_PAD + t, :] = jnp.zeros((CONV_PAD, c), f32)
    pad_ref[CONV_PAD:CONV_PAD + t, :] = x_ref[...].astype(f32)
    half = SHORT_CONV // 2
    rows = min(CONV_ROWS, t)
    for c0 in range(0, c, LANES):
        w = w_ref[:, c0:c0 + LANES]
        bias = b_ref[:, c0:c0 + LANES]
        for r0 in range(0, t, rows):
            acc = bias + pad_ref[pl.ds(CONV_PAD - half + r0, rows), c0:c0 + LANES] * w[0:1, :]
            for j in range(1, SHORT_CONV):
                acc += pad_ref[pl.ds(CONV_PAD - half + r0 + j, rows), c0:c0 + LANES] * w[j:j + 1, :]
            y = _silu(acc)
            if n_l2:
                r = lax.rsqrt(jnp.sum(y * y, axis=-1, keepdims=True) + 1e-6)
                y = y * jnp.where(cb < n_q, r * DN_DK ** -0.5, jnp.where(cb < n_l2, r, 1.0))
            o_ref[r0:r0 + rows, c0:c0 + LANES] = y.astype(o_ref.dtype)


def _conv(p, t, col0, n_ch, w, b, n_q, n_l2):
    m = p.shape[0]
    cw = 512
    cb0 = col0 // cw
    return pl.pallas_call(
        functools.partial(_conv_body, t=t, n_q=n_q, n_l2=n_l2),
        grid=(m // t, n_ch // cw),
        in_specs=[pl.BlockSpec((t, cw), lambda i, j: (i, cb0 + j)),
                  pl.BlockSpec((SHORT_CONV, cw), lambda i, j: (0, j)),
                  pl.BlockSpec((1, cw), lambda i, j: (0, j))],
        out_specs=pl.BlockSpec((t, cw), lambda i, j: (i, j)),
        out_shape=jax.ShapeDtypeStruct((m, n_ch), bf16),
        scratch_shapes=[pltpu.VMEM((t + 2 * CONV_PAD, cw), f32)],
        compiler_params=_cp(2), name="short_conv",
    )(p, w, b)


def _gqa_body(*refs, tc, tl, rope_q):
    if tl:
        (q_ref, kc_ref, vc_ref, kl_ref, vl_ref, cosq_ref, sinq_ref, cosk_ref, sink_ref,
         qn_ref, kn_ref, o_ref, k_s, v_s) = refs
    else:
        q_ref, kc_ref, vc_ref, qn_ref, kn_ref, o_ref, k_s, v_s = refs

    @pl.when(pl.program_id(2) == 0)
    def _():
        k_s[:, 0:tc] = _rms(kc_ref[...].astype(f32), kn_ref[...]).T.astype(bf16)
        v_s[0:tc, 0:GQA_DIM] = vc_ref[...]
        v_s[:, GQA_DIM:2 * GQA_DIM] = jnp.ones((tc + tl, GQA_DIM), bf16)
        if tl:
            rb = min(512, tl)
            for r0 in range(0, tl, rb):
                kk = _rms(kl_ref[r0:r0 + rb, :].astype(f32), kn_ref[...])
                kk = _rope(kk, cosk_ref[r0:r0 + rb, :], sink_ref[r0:r0 + rb, :], GQA_DIM // 4)
                k_s[:, tc + r0:tc + r0 + rb] = kk.T.astype(bf16)
                v_s[tc + r0:tc + r0 + rb, 0:GQA_DIM] = vl_ref[r0:r0 + rb, :]

    kall = k_s[...]
    vall = v_s[...]
    for r in range(GQA_HEADS // GQA_KV_HEADS):
        q = _rms(q_ref[:, r * GQA_DIM:(r + 1) * GQA_DIM].astype(f32), qn_ref[...])
        if rope_q:
            q = _rope(q, cosq_ref[...], sinq_ref[...], GQA_DIM // 4)
        s = _dot((q * (GQA_DIM ** -0.5 * LOG2E)).astype(bf16), kall)
        p = jnp.exp2(s - jnp.max(s, axis=-1, keepdims=True))
        oe = _dot(p.astype(bf16), vall)
        o = oe[:, 0:GQA_DIM] * (1.0 / oe[:, GQA_DIM:GQA_DIM + 1])
        o_ref[:, r * GQA_DIM:(r + 1) * GQA_DIM] = o.astype(o_ref.dtype)


def _gqa(pq, tq_seq, pc, tc, pl_, tl, rope, qn, kn):
    m = pq.shape[0]
    nb = m // tq_seq
    tq = min(256, tq_seq)
    rep = GQA_HEADS // GQA_KV_HEADS
    qw = rep * GQA_DIM
    qb0 = COL_GQA // qw
    kb0 = (COL_GQA + GQA_HEADS * GQA_DIM) // GQA_DIM
    vb0 = kb0 + GQA_KV_HEADS
    kv_spec = lambda t, b0: pl.BlockSpec((t, GQA_DIM), lambda b, g, i: (b, b0 + g))
    vec = pl.BlockSpec((1, GQA_DIM), lambda b, g, i: (0, 0))
    nq = tq_seq // tq
    in_specs = [pl.BlockSpec((tq, qw), lambda b, g, i: (b * nq + i, qb0 + g)), kv_spec(tc, kb0), kv_spec(tc, vb0)]
    args = [pq, pc, pc]
    if tl:
        cos, sin = rope
        in_specs += [kv_spec(tl, kb0), kv_spec(tl, vb0),
                     pl.BlockSpec((tq, GQA_DIM), lambda b, g, i: (i, 0)), pl.BlockSpec((tq, GQA_DIM), lambda b, g, i: (i, 0)),
                     pl.BlockSpec((tl, GQA_DIM), lambda b, g, i: (0, 0)), pl.BlockSpec((tl, GQA_DIM), lambda b, g, i: (0, 0))]
        args += [pl_, pl_, cos, sin, cos, sin]
    in_specs += [vec, vec]
    args += [qn, kn]
    return pl.pallas_call(
        functools.partial(_gqa_body, tc=tc, tl=tl, rope_q=bool(tl)),
        grid=(nb, GQA_KV_HEADS, nq),
        in_specs=in_specs,
        out_specs=pl.BlockSpec((tq, qw), lambda b, g, i: (b * nq + i, g)),
        out_shape=jax.ShapeDtypeStruct((m, GQA_W), bf16),
        scratch_shapes=[pltpu.VMEM((GQA_DIM, tc + tl), bf16), pltpu.VMEM((tc + tl, 2 * GQA_DIM), bf16)],
        compiler_params=_cp(3), name="gqa_attention",
    )(*args)


def _diff_body(*refs, tc, tl, lam_init):
    if tl:
        (q_ref, kc_ref, vc_ref, kl_ref, vl_ref, cosq_ref, sinq_ref, cosk_ref, sink_ref,
         lam_ref, ng_ref, o_ref, k_s, v_s) = refs
    else:
        q_ref, kc_ref, vc_ref, lam_ref, ng_ref, o_ref, k_s, v_s = refs

    @pl.when(pl.program_id(2) == 0)
    def _():
        k_s[:, 0:tc] = kc_ref[...].astype(f32).T.astype(bf16)
        v_s[0:tc, :] = vc_ref[...]
        if tl:
            rb = min(512, tl)
            for r0 in range(0, tl, rb):
                kk = _rope(kl_ref[r0:r0 + rb, :].astype(f32), cosk_ref[r0:r0 + rb, :], sink_ref[r0:r0 + rb, :],
                           DIFF_DIM // 4)
                k_s[:, tc + r0:tc + r0 + rb] = kk.T.astype(bf16)
                v_s[tc + r0:tc + r0 + rb, :] = vl_ref[r0:r0 + rb, :]

    lv = lam_ref[...]
    lam = (jnp.exp(jnp.sum(lv[0:1, :] * lv[1:2, :], axis=-1, keepdims=True))
           - jnp.exp(jnp.sum(lv[2:3, :] * lv[3:4, :], axis=-1, keepdims=True)) + lam_init)
    q = q_ref[...].astype(f32)
    if tl:
        q = _rope(q, cosq_ref[...], sinq_ref[...], DIFF_DIM // 4)
    q = q * (DIFF_DIM ** -0.5 * LOG2E)
    lane = lax.broadcasted_iota(jnp.int32, q.shape, 1)
    kall = k_s[...]
    ps, ls = [], []
    for i in range(2):
        qi = jnp.where((lane < DIFF_DIM) if i == 0 else (lane >= DIFF_DIM), q, 0.0).astype(bf16)
        s = _dot(qi, kall)
        p = jnp.exp2(s - jnp.max(s, axis=-1, keepdims=True))
        ps.append(p)
        ls.append(jnp.sum(p, axis=-1, keepdims=True))
    inv0 = 1.0 / ls[0]
    a = ps[0] - (lam * ls[0] * (1.0 / ls[1])) * ps[1]
    o = _dot(a.astype(bf16), v_s[...]) * inv0
    o_ref[...] = (_rms(o, ng_ref[...]) * (1.0 - lam_init)).astype(o_ref.dtype)


def _diff(pq, tq_seq, pc, tc, pl_, tl, rope, lam_vecs, ng, lam_init):
    m = pq.shape[0]
    nb = m // tq_seq
    tq = min(256, tq_seq)
    nq = tq_seq // tq
    w = DIFF_VDIM
    qb0 = COL_DIFF // w
    kb0 = qb0 + DIFF_HEADS
    vb0 = kb0 + DIFF_HEADS
    kv_spec = lambda t, b0: pl.BlockSpec((t, w), lambda b, h, i: (b, b0 + h))
    in_specs = [pl.BlockSpec((tq, w), lambda b, h, i: (b * nq + i, qb0 + h)), kv_spec(tc, kb0), kv_spec(tc, vb0)]
    args = [pq, pc, pc]
    if tl:
        cos, sin = rope
        in_specs += [kv_spec(tl, kb0), kv_spec(tl, vb0),
                     pl.BlockSpec((tq, w), lambda b, h, i: (i, 0)), pl.BlockSpec((tq, w), lambda b, h, i: (i, 0)),
                     pl.BlockSpec((tl, w), lambda b, h, i: (0, 0)), pl.BlockSpec((tl, w), lambda b, h, i: (0, 0))]
        args += [pl_, pl_, cos, sin, cos, sin]
    in_specs += [pl.BlockSpec((4, DIFF_DIM), lambda b, h, i: (0, 0)), pl.BlockSpec((1, w), lambda b, h, i: (0, 0))]
    args += [lam_vecs, ng]
    return pl.pallas_call(
        functools.partial(_diff_body, tc=tc, tl=tl, lam_init=lam_init),
        grid=(nb, DIFF_HEADS, nq),
        in_specs=in_specs,
        out_specs=pl.BlockSpec((tq, w), lambda b, h, i: (b * nq + i, h)),
        out_shape=jax.ShapeDtypeStruct((m, DIFF_W), bf16),
        scratch_shapes=[pltpu.VMEM((w, tc + tl), bf16), pltpu.VMEM((tc + tl, w), bf16)],
        compiler_params=_cp(3), name="diff_attention",
    )(*args)


def _tri_masks(n):
    row = lax.broadcasted_iota(jnp.int32, (n, n), 0)
    col = lax.broadcasted_iota(jnp.int32, (n, n), 1)
    return row >= col, row <= col, row > col, row < col


def _dn_body(*refs, tc, tl, need_ctx):
    (qc_ref, kc_ref, vc_ref, zc_ref, gc_ref, gtc_ref,
     ql_ref, kl_ref, vl_ref, zl_ref, gl_ref, gtl_ref, cp_ref, rp_ref, ng_ref) = refs[:15]
    if need_ctx:
        oc_ref, ol_ref, o_s, s_s = refs[15:]
    else:
        ol_ref, o_s, s_s = refs[15:]
        oc_ref = None
    c = DN_CHUNK
    ge, le, gt, lt = _tri_masks(c)
    incl = (ge, le)
    strict = (gt, lt)
    ge16 = jnp.where(ge, 1.0, 0.0).astype(bf16)
    le16 = jnp.where(le, 1.0, 0.0).astype(bf16)
    tri_col = (ge16, le16)
    tri_row = (le16, ge16)
    eye = (lax.broadcasted_iota(jnp.int32, (c, c), 0) == lax.broadcasted_iota(jnp.int32, (c, c), 1)).astype(f32)
    neg_a_row = -jnp.exp(cp_ref[0:1, :])
    dtb_row = cp_ref[1:2, :]
    neg_a_col = -jnp.exp(rp_ref[:, 0:1])
    dtb_col = rp_ref[:, 1:2]
    s_s[...] = jnp.zeros_like(s_s)

    def run(q_ref, k_ref, v_ref, g_ref, gt_ref, n_chunks, base):
        def step(i, carry):
            dirs = []
            for d in range(2):
                ci = i if d == 0 else n_chunks - 1 - i
                r = pl.multiple_of(ci * c, c)
                graw = g_ref[pl.ds(r, c), :]
                dirs.append(dict(r=r, beta_all=_sigmoid(graw), gdec=neg_a_row * _softplus(graw + dtb_row),
                                 gdec_r=neg_a_col * _softplus(gt_ref[ci] + dtb_col)))
            for d, dd in enumerate(dirs):
                dd["cum"] = _dot_sel_l(tri_col[d], dd["gdec"], 3)
                dd["cum_r"] = _dot_sel_r(dd["gdec_r"], tri_row[d], 3)
                dd["total"] = jnp.sum(dd["gdec"], axis=0, keepdims=True)
            chains = []
            for d, dd in enumerate(dirs):
                for h in range(DN_HEADS):
                    lb = LANE_DN_B + d * DN_HEADS + h
                    la = LANE_DN_A + d * DN_HEADS + h
                    hs = slice(h * DN_DK, (h + 1) * DN_DK)
                    gcol = dd["cum"][:, la:la + 1]
                    ch = dict(d=d, hs=hs, r=dd["r"], si=d * DN_HEADS + h, beta=dd["beta_all"][:, lb:lb + 1], gcol=gcol,
                              tot=dd["total"][:, la:la + 1], eg=jnp.exp(gcol),
                              decay=jnp.where(incl[d], jnp.exp(jnp.where(incl[d], gcol - dd["cum_r"][la:la + 1, :], 0.0)), 0.0),
                              q16=q_ref[pl.ds(dd["r"], c), hs], k16=k_ref[pl.ds(dd["r"], c), hs],
                              vf=v_ref[pl.ds(dd["r"], c), hs].astype(f32))
                    ch["kf"] = ch["k16"].astype(f32)
                    chains.append(ch)
            for ch in chains:
                ch["kk"] = _dot_nt(ch["k16"], ch["k16"])
            for ch in chains:
                ch["qk"] = (_dot_nt(ch["q16"], ch["k16"]) * ch["decay"]).astype(bf16)
            for ch in chains:
                ch["m"] = -jnp.where(strict[ch["d"]], ch["beta"] * ch["kk"] * ch["decay"], 0.0)
                ch["inv"] = eye + ch["m"]
            for _ in range(int(math.log2(c)) - 1):
                for ch in chains:
                    m16 = ch["m"].astype(bf16)
                    ch["m"] = _dot(m16, m16)
                for ch in chains:
                    ch["inv"] = _dot(ch["inv"].astype(bf16), (eye + ch["m"]).astype(bf16))
            for ch in chains:
                ch["inv16"] = ch["inv"].astype(bf16)
                ch["u"] = _dot(ch["inv16"], (ch["vf"] * ch["beta"]).astype(bf16))
            for ch in chains:
                ch["w"] = _dot(ch["inv16"], (ch["kf"] * (ch["beta"] * ch["eg"])).astype(bf16)).astype(bf16)
            for ch in chains:
                ch["s"] = s_s[ch["si"]]
                ch["s16"] = ch["s"].astype(bf16)
                ch["v16"] = (ch["u"] - _dot(ch["w"], ch["s16"])).astype(bf16)
            for ch in chains:
                ch["o"] = _dot((ch["q16"].astype(f32) * ch["eg"]).astype(bf16), ch["s16"]) + _dot(ch["qk"], ch["v16"])
            for ch in chains:
                kd = (ch["kf"] * jnp.exp(ch["tot"] - ch["gcol"])).astype(bf16)
                s_s[ch["si"]] = ch["s"] * jnp.exp(ch["tot"]) + _dot_tn(kd, ch["v16"])
            for ch in chains:
                o_s[ch["d"], pl.ds(base + ch["r"], c), ch["hs"]] = ch["o"]
            return carry

        lax.fori_loop(0, n_chunks, step, 0)

    run(qc_ref, kc_ref, vc_ref, gc_ref, gtc_ref, tc // c, 0)
    run(ql_ref, kl_ref, vl_ref, gl_ref, gtl_ref, tl // c, tc)

    def finish(z_ref, out_ref, base, t):
        rb = min(256, t)
        for r0 in range(0, t, rb):
            for h in range(DN_HEADS):
                hs = slice(h * DN_DV, (h + 1) * DN_DV)
                o = o_s[0, base + r0:base + r0 + rb, hs] + o_s[1, base + r0:base + r0 + rb, hs]
                out_ref[r0:r0 + rb, hs] = (_rms(o, ng_ref[...]) * _silu(z_ref[r0:r0 + rb, hs].astype(f32))).astype(out_ref.dtype)

    finish(zl_ref, ol_ref, tc, tl)
    if need_ctx:
        finish(zc_ref, oc_ref, 0, tc)


def _deltanet(qkv_c, p_c, ps_c, gt_c, tc, qkv_l, p_l, ps_l, gt_l, tl, cparams, rparams, ng, need_ctx):
    nb = qkv_l.shape[0] // tl
    w = DN_W
    c = DN_CHUNK

    def seq_specs(t):
        return [pl.BlockSpec((t, w), lambda b: (b, 0)), pl.BlockSpec((t, w), lambda b: (b, 1)),
                pl.BlockSpec((t, w), lambda b: (b, 2)),
                pl.BlockSpec((t, w), lambda b: (b, COL_DN_Z // w)),
                pl.BlockSpec((t, LANES), lambda b: (b, 0)),
                pl.BlockSpec((None, t // c, 32, c), lambda b: (b, 0, 0, 0))]

    in_specs = seq_specs(tc) + seq_specs(tl) + [
        pl.BlockSpec((8, LANES), lambda b: (0, 0)), pl.BlockSpec((32, 2), lambda b: (0, 0)),
        pl.BlockSpec((1, DN_DV), lambda b: (0, 0))]
    out_l = jax.ShapeDtypeStruct((nb * tl, w), bf16)
    out_c = jax.ShapeDtypeStruct((nb * tc, w), bf16)
    spec_l = pl.BlockSpec((tl, w), lambda b: (b, 0))
    spec_c = pl.BlockSpec((tc, w), lambda b: (b, 0))
    res = pl.pallas_call(
        functools.partial(_dn_body, tc=tc, tl=tl, need_ctx=need_ctx),
        grid=(nb,),
        in_specs=in_specs,
        out_specs=(spec_c, spec_l) if need_ctx else spec_l,
        out_shape=(out_c, out_l) if need_ctx else out_l,
        scratch_shapes=[pltpu.VMEM((2, tc + tl, w), f32), pltpu.VMEM((2 * DN_HEADS, DN_DK, DN_DV), f32)],
        compiler_params=_cp(1), name="gated_deltanet",
    )(qkv_c, qkv_c, qkv_c, p_c, ps_c, gt_c, qkv_l, qkv_l, qkv_l, p_l, ps_l, gt_l, cparams, rparams, ng)
    return (res[1], res[0]) if need_ctx else (res, None)


def _ssd_body(*refs, tc, tl, need_ctx):
    (xc_ref, zc_ref, gc_ref, gtc_ref, xl_ref, zl_ref, gl_ref, gtl_ref,
     cp_ref, rp_ref, ex_ref, dsk_ref, ng_ref) = refs[:13]
    if need_ctx:
        oc_ref, ol_ref, y_s, s_s = refs[13:]
    else:
        ol_ref, y_s, s_s = refs[13:]
        oc_ref = None
    c = SSD_CHUNK
    ge, le, _, _ = _tri_masks(c)
    incl = (ge, le)
    ge16 = jnp.where(ge, 1.0, 0.0).astype(bf16)
    le16 = jnp.where(le, 1.0, 0.0).astype(bf16)
    tri_col = (ge16, le16)
    tri_row = (le16, ge16)
    neg_a_row = -jnp.exp(cp_ref[0:1, :])
    dtb_row = cp_ref[1:2, :]
    neg_a_col = -jnp.exp(rp_ref[:, 0:1])
    dtb_col = rp_ref[:, 1:2]
    lane = lax.broadcasted_iota(jnp.int32, (c, LANES), 1)
    lo = lane < SSD_HEADDIM
    hi = lane >= SSD_HEADDIM
    s_s[...] = jnp.zeros_like(s_s)
    hpg = SSD_HEADS // SSD_GROUPS
    gw = hpg * SSD_HEADDIM
    b0 = SSD_W
    c0 = SSD_W + SSD_GROUPS * SSD_STATE

    def run(x_ref, g_ref, gt_ref, n_chunks, base):
        def step(i, carry):
            dirs = []
            for d in range(2):
                ci = i if d == 0 else n_chunks - 1 - i
                r = pl.multiple_of(ci * c, c)
                dt = _softplus(g_ref[pl.ds(r, c), :] + dtb_row)
                dirs.append(dict(r=r, dt=dt, da=dt * neg_a_row, da_r=_softplus(gt_ref[ci] + dtb_col) * neg_a_col))
            for d, dd in enumerate(dirs):
                dd["cum"] = _dot_sel_l(tri_col[d], dd["da"], 3)
                dd["cum_r"] = _dot_sel_r(dd["da_r"], tri_row[d], 3)
                dd["total"] = jnp.sum(dd["da"], axis=0, keepdims=True)
            for d, dd in enumerate(dirs):
                stacked = jnp.concatenate([dd["dt"], dd["dt"] * jnp.exp(dd["total"] - dd["cum"]), jnp.exp(dd["cum"]),
                                           jnp.broadcast_to(jnp.exp(dd["total"]), (8, LANES))], axis=0)
                e_all = _dot_sel_r(stacked, ex_ref[d], 2)
                xs = x_ref[pl.ds(dd["r"], c), 0:SSD_W].astype(f32)
                dd["xdt"] = xs * e_all[0:c, :]
                dd["xdtw"] = (xs * e_all[c:2 * c, :]).astype(bf16)
                dd["cin_e"] = e_all[2 * c:3 * c, :]
                dd["cd_e"] = e_all[3 * c:3 * c + 1, :]
            groups = []
            for d, dd in enumerate(dirs):
                for g in range(SSD_GROUPS):
                    groups.append(dict(d=d, g=g, dd=dd, si=d * SSD_GROUPS + g, gs=slice(g * gw, (g + 1) * gw),
                                       bm=x_ref[pl.ds(dd["r"], c), b0 + g * SSD_STATE:b0 + (g + 1) * SSD_STATE],
                                       cm=x_ref[pl.ds(dd["r"], c), c0 + g * SSD_STATE:c0 + (g + 1) * SSD_STATE]))
            for gg in groups:
                gg["scores"] = _dot_nt(gg["cm"], gg["bm"])
            for gg in groups:
                gg["s"] = s_s[gg["si"]]
                gg["y_off"] = _dot(gg["cm"], gg["s"].astype(bf16)) * gg["dd"]["cin_e"][:, gg["gs"]]
            for gg in groups:
                s_s[gg["si"]] = gg["s"] * gg["dd"]["cd_e"][:, gg["gs"]] + _dot_tn(gg["bm"], gg["dd"]["xdtw"][:, gg["gs"]])
            items = []
            for gg in groups:
                d, g, dd = gg["d"], gg["g"], gg["dd"]
                for pair in range(hpg // 2):
                    ps = slice(g * gw + pair * LANES, g * gw + (pair + 1) * LANES)
                    xp = dd["xdt"][:, ps]
                    for e in range(2):
                        la = LANE_SSD_DT + d * SSD_HEADS + g * hpg + pair * 2 + e
                        seg = dd["cum"][:, la:la + 1] - dd["cum_r"][la:la + 1, :]
                        decay = jnp.where(incl[d], jnp.exp(jnp.where(incl[d], seg, 0.0)), 0.0)
                        items.append(dict(m=(gg["scores"] * decay).astype(bf16),
                                          xh=jnp.where(lo if e == 0 else hi, xp, 0.0).astype(bf16)))
                    gg.setdefault("pairs", []).append((ps, pair, len(items) - 2))
            for it in items:
                it["y"] = _dot(it["m"], it["xh"])
            for gg in groups:
                for ps, pair, k in gg["pairs"]:
                    yp = gg["y_off"][:, pair * LANES:(pair + 1) * LANES] + items[k]["y"] + items[k + 1]["y"]
                    y_s[gg["d"], pl.ds(base + gg["dd"]["r"], c), ps] = yp
            return carry

        lax.fori_loop(0, n_chunks, step, 0)

    run(xc_ref, gc_ref, gtc_ref, tc // c, 0)
    run(xl_ref, gl_ref, gtl_ref, tl // c, tc)

    def finish(x_ref, z_ref, out_ref, base, t):
        rb = min(256, t)
        gcols = SSD_W // SSD_GROUPS
        for r0 in range(0, t, rb):
            for g in range(SSD_GROUPS):
                gs = slice(g * gcols, (g + 1) * gcols)
                y = y_s[0, base + r0:base + r0 + rb, gs] + y_s[1, base + r0:base + r0 + rb, gs]
                y = (y + dsk_ref[:, gs] * x_ref[r0:r0 + rb, gs].astype(f32)) * _silu(z_ref[r0:r0 + rb, gs].astype(f32))
                out_ref[r0:r0 + rb, gs] = _rms(y, ng_ref[:, gs]).astype(out_ref.dtype)

    finish(xl_ref, zl_ref, ol_ref, tc, tl)
    if need_ctx:
        finish(xc_ref, zc_ref, oc_ref, 0, tc)


def _ssd(xbc_c, p_c, ps_c, gt_c, tc, xbc_l, p_l, ps_l, gt_l, tl, cparams, rparams, expand, d_skip, ng, need_ctx):
    nb = xbc_l.shape[0] // tl
    w = SSD_W
    c = SSD_CHUNK

    def seq_specs(t):
        return [pl.BlockSpec((t, SSD_CONV_CH), lambda b: (b, 0)),
                pl.BlockSpec((t, w), lambda b: (b, COL_SSD_Z // w)),
                pl.BlockSpec((t, LANES), lambda b: (b, 0)),
                pl.BlockSpec((None, t // c, 32, c), lambda b: (b, 0, 0, 0))]

    in_specs = seq_specs(tc) + seq_specs(tl) + [
        pl.BlockSpec((8, LANES), lambda b: (0, 0)), pl.BlockSpec((32, 2), lambda b: (0, 0)),
        pl.BlockSpec((2, LANES, w), lambda b: (0, 0, 0)),
        pl.BlockSpec((1, w), lambda b: (0, 0)), pl.BlockSpec((1, w), lambda b: (0, 0))]
    out_l = jax.ShapeDtypeStruct((nb * tl, w), bf16)
    out_c = jax.ShapeDtypeStruct((nb * tc, w), bf16)
    spec_l = pl.BlockSpec((tl, w), lambda b: (b, 0))
    spec_c = pl.BlockSpec((tc, w), lambda b: (b, 0))
    res = pl.pallas_call(
        functools.partial(_ssd_body, tc=tc, tl=tl, need_ctx=need_ctx),
        grid=(nb,),
        in_specs=in_specs,
        out_specs=(spec_c, spec_l) if need_ctx else spec_l,
        out_shape=(out_c, out_l) if need_ctx else out_l,
        scratch_shapes=[pltpu.VMEM((2, tc + tl, w), f32),
                        pltpu.VMEM((2 * SSD_GROUPS, SSD_STATE, w // SSD_GROUPS), f32)],
        compiler_params=_cp(1), name="ssd_scan",
    )(xbc_c, p_c, ps_c, gt_c, xbc_l, p_l, ps_l, gt_l, cparams, rparams, expand, d_skip, ng)
    return (res[1], res[0]) if need_ctx else (res, None)


def _rope_tables(n_tok, head_dim, reps):
    rows = n_tok // GRID_W
    row = jnp.broadcast_to(jnp.arange(rows)[:, None], (rows, GRID_W)).reshape(-1).astype(f32)
    col = jnp.broadcast_to(jnp.arange(GRID_W)[None, :], (rows, GRID_W)).reshape(-1).astype(f32)
    half = head_dim // 2
    inv = ROPE_THETA ** (-jnp.arange(0, half, 2, dtype=f32) / half)
    ar = row[:, None] * inv
    ac = col[:, None] * inv
    cos = jnp.concatenate([jnp.cos(ar), jnp.cos(ar), jnp.cos(ac), jnp.cos(ac)], axis=-1)
    sin = jnp.concatenate([-jnp.sin(ar), jnp.sin(ar), -jnp.sin(ac), jnp.sin(ac)], axis=-1)
    return jnp.tile(cos, (1, reps)), jnp.tile(sin, (1, reps))


def _pack_w_in(w_in):
    a_small = DN_CONV_CH + DN_W
    b0 = a_small + 4 * DN_HEADS
    c_small = b0 + (GQA_HEADS + 2 * GQA_KV_HEADS) * GQA_DIM + SSD_W + SSD_CONV_CH
    d0 = c_small + 2 * SSD_HEADS
    d1 = d0 + 4 * DIFF_HEADS * DIFF_DIM + DIFF_W
    w16 = w_in.astype(bf16)
    main = jnp.concatenate([w16[..., 0:a_small], w16[..., b0:c_small], w16[..., d0:d1]], axis=-1)
    pad = jnp.zeros(w_in.shape[:2] + (LANES - 32,), bf16)
    small = jnp.concatenate([w16[..., a_small:b0], w16[..., c_small:d0], pad], axis=-1)
    return main, small


def _gates_rowlayout(ps, t, chunk):
    g = ps[:, 0:32]
    nb = g.shape[0] // t
    return g.reshape(nb, t // chunk, chunk, 32).transpose(0, 1, 3, 2)


def _lane_params(a_log, dt_bias, lane0):
    n = a_log.size
    out = jnp.zeros((8, LANES), f32)
    out = out.at[0, lane0:lane0 + n].set(a_log.reshape(-1))
    return out.at[1, lane0:lane0 + n].set(dt_bias.reshape(-1))


def _row_params(a_log, dt_bias, row0):
    n = a_log.size
    out = jnp.zeros((32, 2), f32)
    out = out.at[row0:row0 + n, 0].set(a_log.reshape(-1))
    return out.at[row0:row0 + n, 1].set(dt_bias.reshape(-1))


def _ssd_expand():
    lane = jnp.arange(LANES)[:, None]
    colh = (jnp.arange(SSD_W) // SSD_HEADDIM)[None, :]
    return jnp.stack([(lane == LANE_SSD_DT + d * SSD_HEADS + colh).astype(bf16) for d in range(2)])


def kernel(x, c, ctx, c_ctx, w_mod, b_mod, ln_g, ln_b, ffn_wg, ffn_wu, ffn_wd, w_in, w_out, dn_conv, dn_a_log, dn_dt_bias, dn_norm_g, gqa_q_norm, gqa_k_norm, ssd_conv, ssd_conv_b, ssd_a_log, ssd_dt_bias, ssd_d, ssd_norm_g, diff_lambda, diff_norm_g):
    nb, tl, d = x.shape
    tc = ctx.shape[1]
    n_layers = w_mod.shape[0]
    xl = x.reshape(nb * tl, d)
    xc = ctx.reshape(nb * tc, d)

    rows = -(-(nb + 1) // 8) * 8
    cvec = jnp.zeros((rows, d), f32).at[:nb].set(c).at[nb].set(c_ctx)
    mods = _modulation(cvec, w_mod, b_mod).reshape(n_layers, rows, N_MOD, d)

    wg16, wu16, wd16 = ffn_wg.astype(bf16), ffn_wu.astype(bf16), ffn_wd.astype(bf16)
    win16, win_small16 = _pack_w_in(w_in)
    wout16 = w_out.astype(bf16)
    lng = ln_g.reshape(n_layers, 3, 1, d)
    lnb = ln_b.reshape(n_layers, 3, 1, d)
    rope_gqa = _rope_tables(tl, GQA_DIM, 1)
    rope_diff = _rope_tables(tl, DIFF_DIM, 2)
    ssd_ex = _ssd_expand()
    zero_bias_dn = jnp.zeros((1, DN_CONV_CH), f32)

    lat = dict(mod_row0=0, rows_per_mod=tl)
    cx = dict(mod_row0=nb, rows_per_mod=nb * tc)

    for l in range(n_layers):
        need_ctx = l < n_layers - 1
        lam_init = 0.8 - 0.6 * math.exp(-0.3 * l)
        ffn = lambda h, grp, base, j, k: _ffn(h, mods, grp["mod_row0"], grp["rows_per_mod"], base,
                                              wg16, wu16, wd16, l, j, lng, lnb, k)
        xl = ffn(xl, lat, 0, 0, 0)
        xc = ffn(xc, cx, 0, 0, 0)

        p_l, ps_l = _inproj(xl, mods, 0, tl, win16, win_small16, l)
        p_c, ps_c = _inproj(xc, mods, nb, nb * tc, win16, win_small16, l)

        dn_qkv_l = _conv(p_l, tl, COL_DN_QKV, DN_CONV_CH, dn_conv[l], zero_bias_dn, 1, 2)
        dn_qkv_c = _conv(p_c, tc, COL_DN_QKV, DN_CONV_CH, dn_conv[l], zero_bias_dn, 1, 2)
        a_l, a_c = _deltanet(dn_qkv_c, p_c, ps_c, _gates_rowlayout(ps_c, tc, DN_CHUNK), tc,
                             dn_qkv_l, p_l, ps_l, _gates_rowlayout(ps_l, tl, DN_CHUNK), tl,
                             _lane_params(dn_a_log[l], dn_dt_bias[l], LANE_DN_A),
                             _row_params(dn_a_log[l], dn_dt_bias[l], LANE_DN_A),
                             dn_norm_g[l].reshape(1, DN_DV), need_ctx)
        qn = gqa_q_norm[l].reshape(1, GQA_DIM)
        kn = gqa_k_norm[l].reshape(1, GQA_DIM)
        b_l = _gqa(p_l, tl, p_c, tc, p_l, tl, rope_gqa, qn, kn)
        b_c = _gqa(p_c, tc, p_c, tc, None, 0, None, qn, kn) if need_ctx else None
        ssd_xbc_l = _conv(p_l, tl, COL_SSD_XBC, SSD_CONV_CH, ssd_conv[l], ssd_conv_b[l].reshape(1, -1), 0, 0)
        ssd_xbc_c = _conv(p_c, tc, COL_SSD_XBC, SSD_CONV_CH, ssd_conv[l], ssd_conv_b[l].reshape(1, -1), 0, 0)
        c_l, c_c = _ssd(ssd_xbc_c, p_c, ps_c, _gates_rowlayout(ps_c, tc, SSD_CHUNK), tc,
                        ssd_xbc_l, p_l, ps_l, _gates_rowlayout(ps_l, tl, SSD_CHUNK), tl,
                        _lane_params(ssd_a_log[l], ssd_dt_bias[l], LANE_SSD_DT),
                        _row_params(ssd_a_log[l], ssd_dt_bias[l], LANE_SSD_DT),
                        ssd_ex, jnp.repeat(ssd_d[l], SSD_HEADDIM).reshape(1, SSD_W),
                        ssd_norm_g[l].reshape(1, SSD_W), need_ctx)
        ng = diff_norm_g[l].reshape(1, DIFF_VDIM)
        d_l = _diff(p_l, tl, p_c, tc, p_l, tl, rope_diff, diff_lambda[l], ng, lam_init)
        d_c = _diff(p_c, tc, p_c, tc, None, 0, None, diff_lambda[l], ng, lam_init) if need_ctx else None

        xl = _outproj(xl, (a_l, b_l, c_l, d_l), wout16, mods, 0, tl, l, lng, lnb)
        xl = ffn(xl, lat, 6, 1, 2)
        if need_ctx:
            xc = _outproj(xc, (a_c, b_c, c_c, d_c), wout16, mods, nb, nb * tc, l, lng, lnb)
            xc = ffn(xc, cx, 6, 1, 2)
    return xl.reshape(nb, tl, d)
```

```python
import functools
import math

import jax
import jax.numpy as jnp
from jax import lax
from jax.experimental import pallas as pl
from jax.experimental.pallas import tpu as pltpu

f32 = jnp.float32
bf16 = jnp.bfloat16

D_MODEL = 2048
DEPTH = 2
GRID_W = 64
D_FF = 5632
SHORT_CONV = 5
ROPE_THETA = 10000.0
N_MOD = 9
DN_HEADS, DN_DK, DN_DV, DN_CHUNK = 4, 128, 128, 64
GQA_HEADS, GQA_KV_HEADS, GQA_DIM = 4, 2, 128
SSD_HEADS, SSD_HEADDIM, SSD_GROUPS, SSD_STATE, SSD_CHUNK = 8, 64, 2, 128, 128
DIFF_HEADS, DIFF_DIM, DIFF_VDIM = 4, 64, 128
DN_W = DN_HEADS * DN_DV
GQA_W = GQA_HEADS * GQA_DIM
SSD_W = SSD_HEADS * SSD_HEADDIM
DIFF_W = DIFF_HEADS * DIFF_VDIM
D_MIX = DN_W + GQA_W + SSD_W + DIFF_W
DN_CONV_CH = 2 * DN_HEADS * DN_DK + DN_W
SSD_CONV_CH = SSD_W + 2 * SSD_GROUPS * SSD_STATE
ALPHA = (2 * DEPTH) ** 0.25

COL_DN_QKV = 0
COL_DN_Z = 1536
COL_GQA = 2048
COL_SSD_Z = 3072
COL_SSD_XBC = 3584
COL_DIFF = 4608
D_MAIN = 6144
LANE_DN_B, LANE_DN_A, LANE_SSD_DT = 0, 8, 16
LOG2E = 1.4426950408889634

VMEM_LIMIT = 56 * 1024 * 1024
LANES = 128


def _cp(n_grid):
    return pltpu.CompilerParams(dimension_semantics=("arbitrary",) * n_grid, vmem_limit_bytes=VMEM_LIMIT)


def _dot(a, b):
    return jnp.dot(a, b, preferred_element_type=f32)


def _dot_nt(a, b):
    return lax.dot_general(a, b, (((1,), (1,)), ((), ())), preferred_element_type=f32)


def _dot_tn(a, b):
    return lax.dot_general(a, b, (((0,), (0,)), ((), ())), preferred_element_type=f32)


def _split_bf16(x, parts):
    out = []
    for _ in range(parts - 1):
        hi = x.astype(bf16)
        out.append(hi)
        x = x - hi.astype(f32)
    out.append(x.astype(bf16))
    return out


def _dot_sel_l(sel16, x, parts):
    n = x.shape[1]
    y = _dot(sel16, jnp.concatenate(_split_bf16(x, parts), axis=1))
    return sum(y[:, k * n:(k + 1) * n] for k in range(parts))


def _dot_sel_r(x, sel16, parts):
    m = x.shape[0]
    y = _dot(jnp.concatenate(_split_bf16(x, parts), axis=0), sel16)
    return sum(y[k * m:(k + 1) * m, :] for k in range(parts))


def _sigmoid(x):
    return 1.0 / (1.0 + jnp.exp(-x))


def _silu(x):
    return x * _sigmoid(x)


def _softplus(x):
    return jnp.maximum(x, 0.0) + jnp.log(1.0 + jnp.exp(-jnp.abs(x)))


def _layernorm(h, g, b):
    mu = jnp.mean(h, axis=-1, keepdims=True)
    hc = h - mu
    var = jnp.mean(hc * hc, axis=-1, keepdims=True)
    return hc * lax.rsqrt(var + 1e-5) * g + b


def _rms(x, g):
    return x * lax.rsqrt(jnp.mean(x * x, axis=-1, keepdims=True) + 1e-6) * g


def _rope(x, cos, sin_signed, blk):
    lane = lax.broadcasted_iota(jnp.int32, x.shape, 1)
    first = (lane & blk) == 0
    partner = jnp.where(first, pltpu.roll(x, LANES - blk, 1), pltpu.roll(x, blk, 1))
    return x * cos + partner * sin_signed


def _mod_body(s_ref, w_ref, b_ref, o_ref):
    o_ref[0] = _dot(_silu(s_ref[...]), w_ref[0]) + b_ref[0]


def _modulation(cvec, w_mod, b_mod):
    n_layers, d, n = w_mod.shape
    rows = cvec.shape[0]
    tn = 1024
    return pl.pallas_call(
        _mod_body,
        grid=(n_layers, n // tn),
        in_specs=[pl.BlockSpec((rows, d), lambda l, j: (0, 0)),
                  pl.BlockSpec((1, d, tn), lambda l, j: (l, 0, j)),
                  pl.BlockSpec((1, 1, tn), lambda l, j: (l, 0, j))],
        out_specs=pl.BlockSpec((1, rows, tn), lambda l, j: (l, 0, j)),
        out_shape=jax.ShapeDtypeStruct((n_layers, rows, n), f32),
        compiler_params=_cp(2), name="modulation",
    )(cvec, w_mod, b_mod.reshape(n_layers, 1, n))


def _ffn_body(x_ref, mod_ref, wg_ref, wu_ref, wd_ref, g_ref, b_ref, o_ref, hm_ref, acc_ref, *, base, nf):
    f = pl.program_id(1)

    @pl.when(f == 0)
    def _():
        hm_ref[...] = (x_ref[...] * (1.0 + mod_ref[base + 1:base + 2, :]) + mod_ref[base:base + 1, :]).astype(bf16)
        acc_ref[...] = jnp.zeros_like(acc_ref)

    hm = hm_ref[...]
    g = _dot(hm, wg_ref[...])
    u = _dot(hm, wu_ref[...])
    acc_ref[...] += _dot((_silu(g) * u).astype(bf16), wd_ref[...])

    @pl.when(f == nf - 1)
    def _():
        h = ALPHA * x_ref[...] + (0.5 * mod_ref[base + 2:base + 3, :]) * acc_ref[...]
        o_ref[...] = _layernorm(h, g_ref[...], b_ref[...])


def _row_tile(rows_per_mod):
    tm = min(512, rows_per_mod)
    assert rows_per_mod % tm == 0
    return tm


def _ffn(x, mods, mod_row0, rows_per_mod, base, wg, wu, wd, l, j, ln_g, ln_b, k):
    m, d = x.shape
    tm = _row_tile(rows_per_mod)
    tf = 512
    nf = D_FF // tf
    tpm = rows_per_mod // tm
    return pl.pallas_call(
        functools.partial(_ffn_body, base=base, nf=nf),
        grid=(m // tm, nf),
        in_specs=[pl.BlockSpec((tm, d), lambda i, f: (i, 0)),
                  pl.BlockSpec((None, None, N_MOD, d), lambda i, f: (l, mod_row0 + i // tpm, 0, 0)),
                  pl.BlockSpec((None, None, d, tf), lambda i, f: (l, j, 0, f)),
                  pl.BlockSpec((None, None, d, tf), lambda i, f: (l, j, 0, f)),
                  pl.BlockSpec((None, None, tf, d), lambda i, f: (l, j, f, 0)),
                  pl.BlockSpec((None, None, 1, d), lambda i, f: (l, k, 0, 0)),
                  pl.BlockSpec((None, None, 1, d), lambda i, f: (l, k, 0, 0))],
        out_specs=pl.BlockSpec((tm, d), lambda i, f: (i, 0)),
        out_shape=jax.ShapeDtypeStruct((m, d), f32),
        scratch_shapes=[pltpu.VMEM((tm, d), bf16), pltpu.VMEM((tm, d), f32)],
        compiler_params=_cp(2), name="half_ffn",
    )(x, mods, wg, wu, wd, ln_g, ln_b)


def _inproj_body(x_ref, mod_ref, w_ref, ws_ref, o_ref, os_ref, hm_ref):
    @pl.when(pl.program_id(1) == 0)
    def _():
        hm_ref[...] = (x_ref[...] * (1.0 + mod_ref[4:5, :]) + mod_ref[3:4, :]).astype(bf16)
        os_ref[...] = _dot(hm_ref[...], ws_ref[...])

    o_ref[...] = _dot(hm_ref[...], w_ref[...]).astype(o_ref.dtype)


def _inproj(x, mods, mod_row0, rows_per_mod, w, w_small, l):
    m, d = x.shape
    tm = 1024 if rows_per_mod % 1024 == 0 else _row_tile(rows_per_mod)
    tn = 1024
    tpm = rows_per_mod // tm
    return pl.pallas_call(
        _inproj_body,
        grid=(m // tm, D_MAIN // tn),
        in_specs=[pl.BlockSpec((tm, d), lambda i, n: (i, 0)),
                  pl.BlockSpec((None, None, N_MOD, d), lambda i, n: (l, mod_row0 + i // tpm, 0, 0)),
                  pl.BlockSpec((None, d, tn), lambda i, n: (l, 0, n)),
                  pl.BlockSpec((None, d, LANES), lambda i, n: (l, 0, 0))],
        out_specs=(pl.BlockSpec((tm, tn), lambda i, n: (i, n)), pl.BlockSpec((tm, LANES), lambda i, n: (i, 0))),
        out_shape=(jax.ShapeDtypeStruct((m, D_MAIN), bf16), jax.ShapeDtypeStruct((m, LANES), f32)),
        scratch_shapes=[pltpu.VMEM((tm, d), bf16)],
        compiler_params=_cp(2), name="in_proj",
    )(x, mods, w, w_small)


def _outproj_body(x_ref, a_ref, b_ref, c_ref, d_ref, w_ref, mod_ref, g_ref, be_ref, o_ref):
    y = _dot(a_ref[...], w_ref[0:DN_W, :])
    y += _dot(b_ref[...], w_ref[DN_W:DN_W + GQA_W, :])
    y += _dot(c_ref[...], w_ref[DN_W + GQA_W:DN_W + GQA_W + SSD_W, :])
    y += _dot(d_ref[...], w_ref[DN_W + GQA_W + SSD_W:D_MIX, :])
    h = ALPHA * x_ref[...] + mod_ref[5:6, :] * y
    o_ref[...] = _layernorm(h, g_ref[...], be_ref[...])


def _outproj(x, outs, w, mods, mod_row0, rows_per_mod, l, ln_g, ln_b):
    m, d = x.shape
    tm = _row_tile(rows_per_mod)
    tpm = rows_per_mod // tm
    mix_specs = [pl.BlockSpec((tm, o.shape[1]), lambda i: (i, 0)) for o in outs]
    return pl.pallas_call(
        _outproj_body,
        grid=(m // tm,),
        in_specs=[pl.BlockSpec((tm, d), lambda i: (i, 0))] + mix_specs + [
            pl.BlockSpec((None, D_MIX, d), lambda i: (l, 0, 0)),
            pl.BlockSpec((None, None, N_MOD, d), lambda i: (l, mod_row0 + i // tpm, 0, 0)),
            pl.BlockSpec((None, None, 1, d), lambda i: (l, 1, 0, 0)),
            pl.BlockSpec((None, None, 1, d), lambda i: (l, 1, 0, 0))],
        out_specs=pl.BlockSpec((tm, d), lambda i: (i, 0)),
        out_shape=jax.ShapeDtypeStruct((m, d), f32),
        compiler_params=_cp(1), name="out_proj",
    )(x, *outs, w, mods, ln_g, ln_b)


CONV_ROWS = 128
CONV_PAD = 8


def _conv_body(x_ref, w_ref, b_ref, o_ref, pad_ref, *, t, n_q, n_l2):
    cb = pl.program_id(1)
    c = x_ref.shape[1]
    pad_ref[0:CONV_PAD, :] = jnp.zeros((CONV_PAD, c), f32)
    pad_ref[CONV_PAD + t:2 * CONV_PAD + t, :] = jnp.zeros((CONV_PAD, c), f32)
    pad_ref[CONV_PAD:CONV_PAD + t, :] = x_ref[...].astype(f32)
    half = SHORT_CONV // 2
    rows = min(CONV_ROWS, t)
    for c0 in range(0, c, LANES):
        w = w_ref[:, c0:c0 + LANES]
        bias = b_ref[:, c0:c0 + LANES]
        for r0 in range(0, t, rows):
            acc = bias + pad_ref[pl.ds(CONV_PAD - half + r0, rows), c0:c0 + LANES] * w[0:1, :]
            for j in range(1, SHORT_CONV):
                acc += pad_ref[pl.ds(CONV_PAD - half + r0 + j, rows), c0:c0 + LANES] * w[j:j + 1, :]
            y = _silu(acc)
            if n_l2:
                r = lax.rsqrt(jnp.sum(y * y, axis=-1, keepdims=True) + 1e-6)
                y = y * jnp.where(cb < n_q, r * DN_DK ** -0.5, jnp.where(cb < n_l2, r, 1.0))
            o_ref[r0:r0 + rows, c0:c0 + LANES] = y.astype(o_ref.dtype)


def _conv(p, t, col0, n_ch, w, b, n_q, n_l2):
    m = p.shape[0]
    cw = 512
    cb0 = col0 // cw
    return pl.pallas_call(
        functools.partial(_conv_body, t=t, n_q=n_q, n_l2=n_l2),
        grid=(m // t, n_ch // cw),
        in_specs=[pl.BlockSpec((t, cw), lambda i, j: (i, cb0 + j)),
                  pl.BlockSpec((SHORT_CONV, cw), lambda i, j: (0, j)),
                  pl.BlockSpec((1, cw), lambda i, j: (0, j))],
        out_specs=pl.BlockSpec((t, cw), lambda i, j: (i, j)),
        out_shape=jax.ShapeDtypeStruct((m, n_ch), bf16),
        scratch_shapes=[pltpu.VMEM((t + 2 * CONV_PAD, cw), f32)],
        compiler_params=_cp(2), name="short_conv",
    )(p, w, b)


def _attend_transposed(k_s, vt_s, qt16, tk):
    n = 3 if tk % (3 * LANES) == 0 else 1
    ck = tk // n
    s_cur = _dot(k_s[0:ck, :], qt16)
    m = l = acc = None
    for c in range(n):
        s_next = _dot(k_s[(c + 1) * ck:(c + 2) * ck, :], qt16) if c + 1 < n else None
        mc = jnp.max(s_cur, axis=0, keepdims=True)
        m_new = mc if c == 0 else jnp.maximum(m, mc)
        p = jnp.exp2(s_cur - m_new)
        lc = jnp.sum(p, axis=0, keepdims=True)
        pv = _dot(vt_s[:, c * ck:(c + 1) * ck], p.astype(bf16))
        if c == 0:
            l, acc = lc, pv
        else:
            alpha = jnp.exp2(m - m_new)
            l = alpha * l + lc
            acc = alpha * acc + pv
        m, s_cur = m_new, s_next
    return acc, l


def _gqa_body(*refs, tc, tl, rope_q):
    if tl:
        (q_ref, kc_ref, vc_ref, kl_ref, vl_ref, cosq_ref, sinq_ref, cosk_ref, sink_ref,
         qn_ref, kn_ref, o_ref, k_s, v_s) = refs
    else:
        q_ref, kc_ref, vc_ref, qn_ref, kn_ref, o_ref, k_s, v_s = refs

    @pl.when(pl.program_id(2) == 0)
    def _():
        k_s[0:tc, :] = _rms(kc_ref[...].astype(f32), kn_ref[...]).astype(bf16)
        v_s[:, 0:tc] = vc_ref[...].astype(f32).T.astype(bf16)
        if tl:
            rb = min(512, tl)
            for r0 in range(0, tl, rb):
                kk = _rms(kl_ref[r0:r0 + rb, :].astype(f32), kn_ref[...])
                kk = _rope(kk, cosk_ref[r0:r0 + rb, :], sink_ref[r0:r0 + rb, :], GQA_DIM // 4)
                k_s[tc + r0:tc + r0 + rb, :] = kk.astype(bf16)
                v_s[:, tc + r0:tc + r0 + rb] = vl_ref[r0:r0 + rb, :].astype(f32).T.astype(bf16)

    tq = q_ref.shape[0]
    rep = GQA_HEADS // GQA_KV_HEADS
    qts = []
    for r in range(rep):
        q = _rms(q_ref[:, r * GQA_DIM:(r + 1) * GQA_DIM].astype(f32), qn_ref[...])
        if rope_q:
            q = _rope(q, cosq_ref[...], sinq_ref[...], GQA_DIM // 4)
        qts.append((q * (GQA_DIM ** -0.5 * LOG2E)).T)
    acc, l = _attend_transposed(k_s, v_s, jnp.concatenate(qts, axis=1).astype(bf16), tc + tl)
    o_t = acc * (1.0 / l)
    for r in range(rep):
        o_ref[:, r * GQA_DIM:(r + 1) * GQA_DIM] = o_t[:, r * tq:(r + 1) * tq].T.astype(o_ref.dtype)


def _gqa(pq, tq_seq, pc, tc, pl_, tl, rope, qn, kn):
    m = pq.shape[0]
    nb = m // tq_seq
    tq = min(256, tq_seq)
    rep = GQA_HEADS // GQA_KV_HEADS
    qw = rep * GQA_DIM
    qb0 = COL_GQA // qw
    kb0 = (COL_GQA + GQA_HEADS * GQA_DIM) // GQA_DIM
    vb0 = kb0 + GQA_KV_HEADS
    kv_spec = lambda t, b0: pl.BlockSpec((t, GQA_DIM), lambda b, g, i: (b, b0 + g))
    vec = pl.BlockSpec((1, GQA_DIM), lambda b, g, i: (0, 0))
    nq = tq_seq // tq
    in_specs = [pl.BlockSpec((tq, qw), lambda b, g, i: (b * nq + i, qb0 + g)), kv_spec(tc, kb0), kv_spec(tc, vb0)]
    args = [pq, pc, pc]
    if tl:
        cos, sin = rope
        in_specs += [kv_spec(tl, kb0), kv_spec(tl, vb0),
                     pl.BlockSpec((tq, GQA_DIM), lambda b, g, i: (i, 0)), pl.BlockSpec((tq, GQA_DIM), lambda b, g, i: (i, 0)),
                     pl.BlockSpec((tl, GQA_DIM), lambda b, g, i: (0, 0)), pl.BlockSpec((tl, GQA_DIM), lambda b, g, i: (0, 0))]
        args += [pl_, pl_, cos, sin, cos, sin]
    in_specs += [vec, vec]
    args += [qn, kn]
    return pl.pallas_call(
        functools.partial(_gqa_body, tc=tc, tl=tl, rope_q=bool(tl)),
        grid=(nb, GQA_KV_HEADS, nq),
        in_specs=in_specs,
        out_specs=pl.BlockSpec((tq, qw), lambda b, g, i: (b * nq + i, g)),
        out_shape=jax.ShapeDtypeStruct((m, GQA_W), bf16),
        scratch_shapes=[pltpu.VMEM((tc + tl, GQA_DIM), bf16), pltpu.VMEM((GQA_DIM, tc + tl), bf16)],
        compiler_params=_cp(3), name="gqa_attention",
    )(*args)


def _diff_body(*refs, tc, tl, lam_init):
    if tl:
        (q_ref, kc_ref, vc_ref, kl_ref, vl_ref, cosq_ref, sinq_ref, cosk_ref, sink_ref,
         lam_ref, ng_ref, o_ref, k_s, v_s) = refs
    else:
        q_ref, kc_ref, vc_ref, lam_ref, ng_ref, o_ref, k_s, v_s = refs

    @pl.when(pl.program_id(2) == 0)
    def _():
        k_s[0:tc, :] = kc_ref[...]
        v_s[:, 0:tc] = vc_ref[...].astype(f32).T.astype(bf16)
        if tl:
            rb = min(512, tl)
            for r0 in range(0, tl, rb):
                kk = _rope(kl_ref[r0:r0 + rb, :].astype(f32), cosk_ref[r0:r0 + rb, :], sink_ref[r0:r0 + rb, :],
                           DIFF_DIM // 4)
                k_s[tc + r0:tc + r0 + rb, :] = kk.astype(bf16)
                v_s[:, tc + r0:tc + r0 + rb] = vl_ref[r0:r0 + rb, :].astype(f32).T.astype(bf16)

    lv = lam_ref[...]
    lam = (jnp.exp(jnp.sum(lv[0:1, :] * lv[1:2, :], axis=-1, keepdims=True))
           - jnp.exp(jnp.sum(lv[2:3, :] * lv[3:4, :], axis=-1, keepdims=True)) + lam_init)
    q = q_ref[...].astype(f32)
    if tl:
        q = _rope(q, cosq_ref[...], sinq_ref[...], DIFF_DIM // 4)
    tq = q.shape[0]
    qt = (q * (DIFF_DIM ** -0.5 * LOG2E)).T
    row = lax.broadcasted_iota(jnp.int32, qt.shape, 0)
    qt16 = jnp.concatenate([jnp.where(row < DIFF_DIM, qt, 0.0), jnp.where(row >= DIFF_DIM, qt, 0.0)],
                           axis=1).astype(bf16)
    acc, l = _attend_transposed(k_s, v_s, qt16, tc + tl)
    o_t = acc * (1.0 / l)
    o = (o_t[:, 0:tq] - lam * o_t[:, tq:2 * tq]).T
    o_ref[...] = (_rms(o, ng_ref[...]) * (1.0 - lam_init)).astype(o_ref.dtype)


def _diff(pq, tq_seq, pc, tc, pl_, tl, rope, lam_vecs, ng, lam_init):
    m = pq.shape[0]
    nb = m // tq_seq
    tq = min(256, tq_seq)
    nq = tq_seq // tq
    w = DIFF_VDIM
    qb0 = COL_DIFF // w
    kb0 = qb0 + DIFF_HEADS
    vb0 = kb0 + DIFF_HEADS
    kv_spec = lambda t, b0: pl.BlockSpec((t, w), lambda b, h, i: (b, b0 + h))
    in_specs = [pl.BlockSpec((tq, w), lambda b, h, i: (b * nq + i, qb0 + h)), kv_spec(tc, kb0), kv_spec(tc, vb0)]
    args = [pq, pc, pc]
    if tl:
        cos, sin = rope
        in_specs += [kv_spec(tl, kb0), kv_spec(tl, vb0),
                     pl.BlockSpec((tq, w), lambda b, h, i: (i, 0)), pl.BlockSpec((tq, w), lambda b, h, i: (i, 0)),
                     pl.BlockSpec((tl, w), lambda b, h, i: (0, 0)), pl.BlockSpec((tl, w), lambda b, h, i: (0, 0))]
        args += [pl_, pl_, cos, sin, cos, sin]
    in_specs += [pl.BlockSpec((4, DIFF_DIM), lambda b, h, i: (0, 0)), pl.BlockSpec((1, w), lambda b, h, i: (0, 0))]
    args += [lam_vecs, ng]
    return pl.pallas_call(
        functools.partial(_diff_body, tc=tc, tl=tl, lam_init=lam_init),
        grid=(nb, DIFF_HEADS, nq),
        in_specs=in_specs,
        out_specs=pl.BlockSpec((tq, w), lambda b, h, i: (b * nq + i, h)),
        out_shape=jax.ShapeDtypeStruct((m, DIFF_W), bf16),
        scratch_shapes=[pltpu.VMEM((tc + tl, w), bf16), pltpu.VMEM((w, tc + tl), bf16)],
        compiler_params=_cp(3), name="diff_attention",
    )(*args)


def _tri_masks(n):
    row = lax.broadcasted_iota(jnp.int32, (n, n), 0)
    col = lax.broadcasted_iota(jnp.int32, (n, n), 1)
    return row >= col, row <= col, row > col, row < col


def _dn_body(*refs, tc, tl, need_ctx):
    (qc_ref, kc_ref, vc_ref, zc_ref, gc_ref, gtc_ref,
     ql_ref, kl_ref, vl_ref, zl_ref, gl_ref, gtl_ref, cp_ref, rp_ref, ng_ref) = refs[:15]
    if need_ctx:
        oc_ref, ol_ref, o_s, s_s = refs[15:]
    else:
        ol_ref, o_s, s_s = refs[15:]
        oc_ref = None
    c = DN_CHUNK
    ge, le, gt, lt = _tri_masks(c)
    incl = (ge, le)
    strict = (gt, lt)
    ge16 = jnp.where(ge, 1.0, 0.0).astype(bf16)
    le16 = jnp.where(le, 1.0, 0.0).astype(bf16)
    tri_col = (ge16, le16)
    tri_row = (le16, ge16)
    eye = (lax.broadcasted_iota(jnp.int32, (c, c), 0) == lax.broadcasted_iota(jnp.int32, (c, c), 1)).astype(f32)
    neg_a_row = -jnp.exp(cp_ref[0:1, :])
    dtb_row = cp_ref[1:2, :]
    neg_a_col = -jnp.exp(rp_ref[:, 0:1])
    dtb_col = rp_ref[:, 1:2]
    s_s[...] = jnp.zeros_like(s_s)

    def run(q_ref, k_ref, v_ref, g_ref, gt_ref, n_chunks, base):
        def step(i, carry):
            dirs = []
            for d in range(2):
                ci = i if d == 0 else n_chunks - 1 - i
                r = pl.multiple_of(ci * c, c)
                graw = g_ref[pl.ds(r, c), :]
                dirs.append(dict(r=r, beta_all=_sigmoid(graw), gdec=neg_a_row * _softplus(graw + dtb_row),
                                 gdec_r=neg_a_col * _softplus(gt_ref[ci] + dtb_col)))
            for d, dd in enumerate(dirs):
                dd["cum"] = _dot_sel_l(tri_col[d], dd["gdec"], 3)
                dd["cum_r"] = _dot_sel_r(dd["gdec_r"], tri_row[d], 3)
                dd["total"] = jnp.sum(dd["gdec"], axis=0, keepdims=True)
            chains = []
            for d, dd in enumerate(dirs):
                for h in range(DN_HEADS):
                    lb = LANE_DN_B + d * DN_HEADS + h
                    la = LANE_DN_A + d * DN_HEADS + h
                    hs = slice(h * DN_DK, (h + 1) * DN_DK)
                    gcol = dd["cum"][:, la:la + 1]
                    ch = dict(d=d, hs=hs, r=dd["r"], si=d * DN_HEADS + h, beta=dd["beta_all"][:, lb:lb + 1], gcol=gcol,
                              tot=dd["total"][:, la:la + 1], eg=jnp.exp(gcol),
                              decay=jnp.where(incl[d], jnp.exp(jnp.where(incl[d], gcol - dd["cum_r"][la:la + 1, :], 0.0)), 0.0),
                              q16=q_ref[pl.ds(dd["r"], c), hs], k16=k_ref[pl.ds(dd["r"], c), hs],
                              vf=v_ref[pl.ds(dd["r"], c), hs].astype(f32))
                    ch["kf"] = ch["k16"].astype(f32)
                    chains.append(ch)
            for ch in chains:
                ch["kk"] = _dot_nt(ch["k16"], ch["k16"])
            for ch in chains:
                ch["qk"] = (_dot_nt(ch["q16"], ch["k16"]) * ch["decay"]).astype(bf16)
            for ch in chains:
                ch["m"] = -jnp.where(strict[ch["d"]], ch["beta"] * ch["kk"] * ch["decay"], 0.0)
                ch["inv"] = eye + ch["m"]
            for _ in range(int(math.log2(c)) - 1):
                for ch in chains:
                    m16 = ch["m"].astype(bf16)
                    ch["m"] = _dot(m16, m16)
                for ch in chains:
                    ch["inv"] = _dot(ch["inv"].astype(bf16), (eye + ch["m"]).astype(bf16))
            for ch in chains:
                ch["inv16"] = ch["inv"].astype(bf16)
                ch["u"] = _dot(ch["inv16"], (ch["vf"] * ch["beta"]).astype(bf16))
            for ch in chains:
                ch["w"] = _dot(ch["inv16"], (ch["kf"] * (ch["beta"] * ch["eg"])).astype(bf16)).astype(bf16)
            for ch in chains:
                ch["s"] = s_s[ch["si"]]
                ch["s16"] = ch["s"].astype(bf16)
                ch["v16"] = (ch["u"] - _dot(ch["w"], ch["s16"])).astype(bf16)
            for ch in chains:
                ch["o"] = _dot((ch["q16"].astype(f32) * ch["eg"]).astype(bf16), ch["s16"]) + _dot(ch["qk"], ch["v16"])
            for ch in chains:
                kd = (ch["kf"] * jnp.exp(ch["tot"] - ch["gcol"])).astype(bf16)
                s_s[ch["si"]] = ch["s"] * jnp.exp(ch["tot"]) + _dot_tn(kd, ch["v16"])
            for ch in chains:
                o_s[ch["d"], pl.ds(base + ch["r"], c), ch["hs"]] = ch["o"]
            return carry

        lax.fori_loop(0, n_chunks, step, 0)

    run(qc_ref, kc_ref, vc_ref, gc_ref, gtc_ref, tc // c, 0)
    run(ql_ref, kl_ref, vl_ref, gl_ref, gtl_ref, tl // c, tc)

    def finish(z_ref, out_ref, base, t):
        rb = min(256, t)
        for r0 in range(0, t, rb):
            for h in range(DN_HEADS):
                hs = slice(h * DN_DV, (h + 1) * DN_DV)
                o = o_s[0, base + r0:base + r0 + rb, hs] + o_s[1, base + r0:base + r0 + rb, hs]
                out_ref[r0:r0 + rb, hs] = (_rms(o, ng_ref[...]) * _silu(z_ref[r0:r0 + rb, hs].astype(f32))).astype(out_ref.dtype)

    finish(zl_ref, ol_ref, tc, tl)
    if need_ctx:
        finish(zc_ref, oc_ref, 0, tc)


def _deltanet(qkv_c, p_c, ps_c, gt_c, tc, qkv_l, p_l, ps_l, gt_l, tl, cparams, rparams, ng, need_ctx):
    nb = qkv_l.shape[0] // tl
    w = DN_W
    c = DN_CHUNK

    def seq_specs(t):
        return [pl.BlockSpec((t, w), lambda b: (b, 0)), pl.BlockSpec((t, w), lambda b: (b, 1)),
                pl.BlockSpec((t, w), lambda b: (b, 2)),
                pl.BlockSpec((t, w), lambda b: (b, COL_DN_Z // w)),
                pl.BlockSpec((t, LANES), lambda b: (b, 0)),
                pl.BlockSpec((None, t // c, 32, c), lambda b: (b, 0, 0, 0))]

    in_specs = seq_specs(tc) + seq_specs(tl) + [
        pl.BlockSpec((8, LANES), lambda b: (0, 0)), pl.BlockSpec((32, 2), lambda b: (0, 0)),
        pl.BlockSpec((1, DN_DV), lambda b: (0, 0))]
    out_l = jax.ShapeDtypeStruct((nb * tl, w), bf16)
    out_c = jax.ShapeDtypeStruct((nb * tc, w), bf16)
    spec_l = pl.BlockSpec((tl, w), lambda b: (b, 0))
    spec_c = pl.BlockSpec((tc, w), lambda b: (b, 0))
    res = pl.pallas_call(
        functools.partial(_dn_body, tc=tc, tl=tl, need_ctx=need_ctx),
        grid=(nb,),
        in_specs=in_specs,
        out_specs=(spec_c, spec_l) if need_ctx else spec_l,
        out_shape=(out_c, out_l) if need_ctx else out_l,
        scratch_shapes=[pltpu.VMEM((2, tc + tl, w), f32), pltpu.VMEM((2 * DN_HEADS, DN_DK, DN_DV), f32)],
        compiler_params=_cp(1), name="gated_deltanet",
    )(qkv_c, qkv_c, qkv_c, p_c, ps_c, gt_c, qkv_l, qkv_l, qkv_l, p_l, ps_l, gt_l, cparams, rparams, ng)
    return (res[1], res[0]) if need_ctx else (res, None)


def _ssd_body(*refs, tc, tl, need_ctx):
    (xc_ref, zc_ref, gc_ref, gtc_ref, xl_ref, zl_ref, gl_ref, gtl_ref,
     cp_ref, rp_ref, ex_ref, dsk_ref, ng_ref) = refs[:13]
    if need_ctx:
        oc_ref, ol_ref, y_s, s_s = refs[13:]
    else:
        ol_ref, y_s, s_s = refs[13:]
        oc_ref = None
    c = SSD_CHUNK
    ge, le, _, _ = _tri_masks(c)
    incl = (ge, le)
    ge16 = jnp.where(ge, 1.0, 0.0).astype(bf16)
    le16 = jnp.where(le, 1.0, 0.0).astype(bf16)
    tri_col = (ge16, le16)
    tri_row = (le16, ge16)
    neg_a_row = -jnp.exp(cp_ref[0:1, :])
    dtb_row = cp_ref[1:2, :]
    neg_a_col = -jnp.exp(rp_ref[:, 0:1])
    dtb_col = rp_ref[:, 1:2]
    lane = lax.broadcasted_iota(jnp.int32, (c, LANES), 1)
    lo = lane < SSD_HEADDIM
    hi = lane >= SSD_HEADDIM
    s_s[...] = jnp.zeros_like(s_s)
    hpg = SSD_HEADS // SSD_GROUPS
    gw = hpg * SSD_HEADDIM
    b0 = SSD_W
    c0 = SSD_W + SSD_GROUPS * SSD_STATE

    def run(x_ref, g_ref, gt_ref, n_chunks, base):
        def step(i, carry):
            dirs = []
            for d in range(2):
                ci = i if d == 0 else n_chunks - 1 - i
                r = pl.multiple_of(ci * c, c)
                dt = _softplus(g_ref[pl.ds(r, c), :] + dtb_row)
                dirs.append(dict(r=r, dt=dt, da=dt * neg_a_row, da_r=_softplus(gt_ref[ci] + dtb_col) * neg_a_col))
            for d, dd in enumerate(dirs):
                dd["cum"] = _dot_sel_l(tri_col[d], dd["da"], 3)
                dd["cum_r"] = _dot_sel_r(dd["da_r"], tri_row[d], 3)
                dd["total"] = jnp.sum(dd["da"], axis=0, keepdims=True)
            for d, dd in enumerate(dirs):
                stacked = jnp.concatenate([dd["dt"], dd["dt"] * jnp.exp(dd["total"] - dd["cum"]), jnp.exp(dd["cum"]),
                                           jnp.broadcast_to(jnp.exp(dd["total"]), (8, LANES))], axis=0)
                e_all = _dot_sel_r(stacked, ex_ref[d], 2)
                xs = x_ref[pl.ds(dd["r"], c), 0:SSD_W].astype(f32)
                dd["xdt"] = xs * e_all[0:c, :]
                dd["xdtw"] = (xs * e_all[c:2 * c, :]).astype(bf16)
                dd["cin_e"] = e_all[2 * c:3 * c, :]
                dd["cd_e"] = e_all[3 * c:3 * c + 1, :]
            groups = []
            for d, dd in enumerate(dirs):
                for g in range(SSD_GROUPS):
                    groups.append(dict(d=d, g=g, dd=dd, si=d * SSD_GROUPS + g, gs=slice(g * gw, (g + 1) * gw),
                                       bm=x_ref[pl.ds(dd["r"], c), b0 + g * SSD_STATE:b0 + (g + 1) * SSD_STATE],
                                       cm=x_ref[pl.ds(dd["r"], c), c0 + g * SSD_STATE:c0 + (g + 1) * SSD_STATE]))
            for gg in groups:
                gg["scores"] = _dot_nt(gg["cm"], gg["bm"])
            for gg in groups:
                gg["s"] = s_s[gg["si"]]
                gg["y_off"] = _dot(gg["cm"], gg["s"].astype(bf16)) * gg["dd"]["cin_e"][:, gg["gs"]]
            for gg in groups:
                s_s[gg["si"]] = gg["s"] * gg["dd"]["cd_e"][:, gg["gs"]] + _dot_tn(gg["bm"], gg["dd"]["xdtw"][:, gg["gs"]])
            items = []
            for gg in groups:
                d, g, dd = gg["d"], gg["g"], gg["dd"]
                for pair in range(hpg // 2):
                    ps = slice(g * gw + pair * LANES, g * gw + (pair + 1) * LANES)
                    xp = dd["xdt"][:, ps]
                    for e in range(2):
                        la = LANE_SSD_DT + d * SSD_HEADS + g * hpg + pair * 2 + e
                        seg = dd["cum"][:, la:la + 1] - dd["cum_r"][la:la + 1, :]
                        decay = jnp.where(incl[d], jnp.exp(jnp.where(incl[d], seg, 0.0)), 0.0)
                        items.append(dict(m=(gg["scores"] * decay).astype(bf16),
                                          xh=jnp.where(lo if e == 0 else hi, xp, 0.0).astype(bf16)))
                    gg.setdefault("pairs", []).append((ps, pair, len(items) - 2))
            for it in items:
                it["y"] = _dot(it["m"], it["xh"])
            for gg in groups:
                for ps, pair, k in gg["pairs"]:
                    yp = gg["y_off"][:, pair * LANES:(pair + 1) * LANES] + items[k]["y"] + items[k + 1]["y"]
                    y_s[gg["d"], pl.ds(base + gg["dd"]["r"], c), ps] = yp
            return carry

        lax.fori_loop(0, n_chunks, step, 0)

    run(xc_ref, gc_ref, gtc_ref, tc // c, 0)
    run(xl_ref, gl_ref, gtl_ref, tl // c, tc)

    def finish(x_ref, z_ref, out_ref, base, t):
        rb = min(256, t)
        gcols = SSD_W // SSD_GROUPS
        for r0 in range(0, t, rb):
            for g in range(SSD_GROUPS):
                gs = slice(g * gcols, (g + 1) * gcols)
                y = y_s[0, base + r0:base + r0 + rb, gs] + y_s[1, base + r0:base + r0 + rb, gs]
                y = (y + dsk_ref[:, gs] * x_ref[r0:r0 + rb, gs].astype(f32)) * _silu(z_ref[r0:r0 + rb, gs].astype(f32))
                out_ref[r0:r0 + rb, gs] = _rms(y, ng_ref[:, gs]).astype(out_ref.dtype)

    finish(xl_ref, zl_ref, ol_ref, tc, tl)
    if need_ctx:
        finish(xc_ref, zc_ref, oc_ref, 0, tc)


def _ssd(xbc_c, p_c, ps_c, gt_c, tc, xbc_l, p_l, ps_l, gt_l, tl, cparams, rparams, expand, d_skip, ng, need_ctx):
    nb = xbc_l.shape[0] // tl
    w = SSD_W
    c = SSD_CHUNK

    def seq_specs(t):
        return [pl.BlockSpec((t, SSD_CONV_CH), lambda b: (b, 0)),
                pl.BlockSpec((t, w), lambda b: (b, COL_SSD_Z // w)),
                pl.BlockSpec((t, LANES), lambda b: (b, 0)),
                pl.BlockSpec((None, t // c, 32, c), lambda b: (b, 0, 0, 0))]

    in_specs = seq_specs(tc) + seq_specs(tl) + [
        pl.BlockSpec((8, LANES), lambda b: (0, 0)), pl.BlockSpec((32, 2), lambda b: (0, 0)),
        pl.BlockSpec((2, LANES, w), lambda b: (0, 0, 0)),
        pl.BlockSpec((1, w), lambda b: (0, 0)), pl.BlockSpec((1, w), lambda b: (0, 0))]
    out_l = jax.ShapeDtypeStruct((nb * tl, w), bf16)
    out_c = jax.ShapeDtypeStruct((nb * tc, w), bf16)
    spec_l = pl.BlockSpec((tl, w), lambda b: (b, 0))
    spec_c = pl.BlockSpec((tc, w), lambda b: (b, 0))
    res = pl.pallas_call(
        functools.partial(_ssd_body, tc=tc, tl=tl, need_ctx=need_ctx),
        grid=(nb,),
        in_specs=in_specs,
        out_specs=(spec_c, spec_l) if need_ctx else spec_l,
        out_shape=(out_c, out_l) if need_ctx else out_l,
        scratch_shapes=[pltpu.VMEM((2, tc + tl, w), f32),
                        pltpu.VMEM((2 * SSD_GROUPS, SSD_STATE, w // SSD_GROUPS), f32)],
        compiler_params=_cp(1), name="ssd_scan",
    )(xbc_c, p_c, ps_c, gt_c, xbc_l, p_l, ps_l, gt_l, cparams, rparams, expand, d_skip, ng)
    return (res[1], res[0]) if need_ctx else (res, None)


def _rope_tables(n_tok, head_dim, reps):
    rows = n_tok // GRID_W
    row = jnp.broadcast_to(jnp.arange(rows)[:, None], (rows, GRID_W)).reshape(-1).astype(f32)
    col = jnp.broadcast_to(jnp.arange(GRID_W)[None, :], (rows, GRID_W)).reshape(-1).astype(f32)
    half = head_dim // 2
    inv = ROPE_THETA ** (-jnp.arange(0, half, 2, dtype=f32) / half)
    ar = row[:, None] * inv
    ac = col[:, None] * inv
    cos = jnp.concatenate([jnp.cos(ar), jnp.cos(ar), jnp.cos(ac), jnp.cos(ac)], axis=-1)
    sin = jnp.concatenate([-jnp.sin(ar), jnp.sin(ar), -jnp.sin(ac), jnp.sin(ac)], axis=-1)
    return jnp.tile(cos, (1, reps)), jnp.tile(sin, (1, reps))


def _pack_w_in(w_in):
    a_small = DN_CONV_CH + DN_W
    b0 = a_small + 4 * DN_HEADS
    c_small = b0 + (GQA_HEADS + 2 * GQA_KV_HEADS) * GQA_DIM + SSD_W + SSD_CONV_CH
    d0 = c_small + 2 * SSD_HEADS
    d1 = d0 + 4 * DIFF_HEADS * DIFF_DIM + DIFF_W
    w16 = w_in.astype(bf16)
    main = jnp.concatenate([w16[..., 0:a_small], w16[..., b0:c_small], w16[..., d0:d1]], axis=-1)
    pad = jnp.zeros(w_in.shape[:2] + (LANES - 32,), bf16)
    small = jnp.concatenate([w16[..., a_small:b0], w16[..., c_small:d0], pad], axis=-1)
    return main, small


def _gates_rowlayout(ps, t, chunk):
    g = ps[:, 0:32]
    nb = g.shape[0] // t
    return g.reshape(nb, t // chunk, chunk, 32).transpose(0, 1, 3, 2)


def _lane_params(a_log, dt_bias, lane0):
    n = a_log.size
    out = jnp.zeros((8, LANES), f32)
    out = out.at[0, lane0:lane0 + n].set(a_log.reshape(-1))
    return out.at[1, lane0:lane0 + n].set(dt_bias.reshape(-1))


def _row_params(a_log, dt_bias, row0):
    n = a_log.size
    out = jnp.zeros((32, 2), f32)
    out = out.at[row0:row0 + n, 0].set(a_log.reshape(-1))
    return out.at[row0:row0 + n, 1].set(dt_bias.reshape(-1))


def _ssd_expand():
    lane = jnp.arange(LANES)[:, None]
    colh = (jnp.arange(SSD_W) // SSD_HEADDIM)[None, :]
    return jnp.stack([(lane == LANE_SSD_DT + d * SSD_HEADS + colh).astype(bf16) for d in range(2)])


def kernel(x, c, ctx, c_ctx, w_mod, b_mod, ln_g, ln_b, ffn_wg, ffn_wu, ffn_wd, w_in, w_out, dn_conv, dn_a_log, dn_dt_bias, dn_norm_g, gqa_q_norm, gqa_k_norm, ssd_conv, ssd_conv_b, ssd_a_log, ssd_dt_bias, ssd_d, ssd_norm_g, diff_lambda, diff_norm_g):
    nb, tl, d = x.shape
    tc = ctx.shape[1]
    n_layers = w_mod.shape[0]
    xl = x.reshape(nb * tl, d)
    xc = ctx.reshape(nb * tc, d)

    rows = -(-(nb + 1) // 8) * 8
    cvec = jnp.zeros((rows, d), f32).at[:nb].set(c).at[nb].set(c_ctx)
    mods = _modulation(cvec, w_mod, b_mod).reshape(n_layers, rows, N_MOD, d)

    wg16, wu16, wd16 = ffn_wg.astype(bf16), ffn_wu.astype(bf16), ffn_wd.astype(bf16)
    win16, win_small16 = _pack_w_in(w_in)
    wout16 = w_out.astype(bf16)
    lng = ln_g.reshape(n_layers, 3, 1, d)
    lnb = ln_b.reshape(n_layers, 3, 1, d)
    rope_gqa = _rope_tables(tl, GQA_DIM, 1)
    rope_diff = _rope_tables(tl, DIFF_DIM, 2)
    ssd_ex = _ssd_expand()
    zero_bias_dn = jnp.zeros((1, DN_CONV_CH), f32)

    lat = dict(mod_row0=0, rows_per_mod=tl)
    cx = dict(mod_row0=nb, rows_per_mod=nb * tc)

    for l in range(n_layers):
        need_ctx = l < n_layers - 1
        lam_init = 0.8 - 0.6 * math.exp(-0.3 * l)
        ffn = lambda h, grp, base, j, k: _ffn(h, mods, grp["mod_row0"], grp["rows_per_mod"], base,
                                              wg16, wu16, wd16, l, j, lng, lnb, k)
        xl = ffn(xl, lat, 0, 0, 0)
        xc = ffn(xc, cx, 0, 0, 0)

        p_l, ps_l = _inproj(xl, mods, 0, tl, win16, win_small16, l)
        p_c, ps_c = _inproj(xc, mods, nb, nb * tc, win16, win_small16, l)

        dn_qkv_l = _conv(p_l, tl, COL_DN_QKV, DN_CONV_CH, dn_conv[l], zero_bias_dn, 1, 2)
        dn_qkv_c = _conv(p_c, tc, COL_DN_QKV, DN_CONV_CH, dn_conv[l], zero_bias_dn, 1, 2)
        a_l, a_c = _deltanet(dn_qkv_c, p_c, ps_c, _gates_rowlayout(ps_c, tc, DN_CHUNK), tc,
                             dn_qkv_l, p_l, ps_l, _gates_rowlayout(ps_l, tl, DN_CHUNK), tl,
                             _lane_params(dn_a_log[l], dn_dt_bias[l], LANE_DN_A),
                             _row_params(dn_a_log[l], dn_dt_bias[l], LANE_DN_A),
                             dn_norm_g[l].reshape(1, DN_DV), need_ctx)
        qn = gqa_q_norm[l].reshape(1, GQA_DIM)
        kn = gqa_k_norm[l].reshape(1, GQA_DIM)
        b_l = _gqa(p_l, tl, p_c, tc, p_l, tl, rope_gqa, qn, kn)
        b_c = _gqa(p_c, tc, p_c, tc, None, 0, None, qn, kn) if need_ctx else None
        ssd_xbc_l = _conv(p_l, tl, COL_SSD_XBC, SSD_CONV_CH, ssd_conv[l], ssd_conv_b[l].reshape(1, -1), 0, 0)
        ssd_xbc_c = _conv(p_c, tc, COL_SSD_XBC, SSD_CONV_CH, ssd_conv[l], ssd_conv_b[l].reshape(1, -1), 0, 0)
        c_l, c_c = _ssd(ssd_xbc_c, p_c, ps_c, _gates_rowlayout(ps_c, tc, SSD_CHUNK), tc,
                        ssd_xbc_l, p_l, ps_l, _gates_rowlayout(ps_l, tl, SSD_CHUNK), tl,
                        _lane_params(ssd_a_log[l], ssd_dt_bias[l], LANE_SSD_DT),
                        _row_params(ssd_a_log[l], ssd_dt_bias[l], LANE_SSD_DT),
                        ssd_ex, jnp.repeat(ssd_d[l], SSD_HEADDIM).reshape(1, SSD_W),
                        ssd_norm_g[l].reshape(1, SSD_W), need_ctx)
        ng = diff_norm_g[l].reshape(1, DIFF_VDIM)
        d_l = _diff(p_l, tl, p_c, tc, p_l, tl, rope_diff, diff_lambda[l], ng, lam_init)
        d_c = _diff(p_c, tc, p_c, tc, None, 0, None, diff_lambda[l], ng, lam_init) if need_ctx else None

        xl = _outproj(xl, (a_l, b_l, c_l, d_l), wout16, mods, 0, tl, l, lng, lnb)
        xl = ffn(xl, lat, 6, 1, 2)
        if need_ctx:
            xc = _outproj(xc, (a_c, b_c, c_c, d_c), wout16, mods, nb, nb * tc, l, lng, lnb)
            xc = ffn(xc, cx, 6, 1, 2)
    return xl.reshape(nb, tl, d)
```

```python
import functools
import math

import jax
import jax.numpy as jnp
from jax import lax
from jax.experimental import pallas as pl
from jax.experimental.pallas import tpu as pltpu

f32 = jnp.float32
bf16 = jnp.bfloat16

D_MODEL = 2048
DEPTH = 2
GRID_W = 64
D_FF = 5632
SHORT_CONV = 5
ROPE_THETA = 10000.0
N_MOD = 9
DN_HEADS, DN_DK, DN_DV, DN_CHUNK = 4, 128, 128, 64
GQA_HEADS, GQA_KV_HEADS, GQA_DIM = 4, 2, 128
SSD_HEADS, SSD_HEADDIM, SSD_GROUPS, SSD_STATE, SSD_CHUNK = 8, 64, 2, 128, 128
DIFF_HEADS, DIFF_DIM, DIFF_VDIM = 4, 64, 128
DN_W = DN_HEADS * DN_DV
GQA_W = GQA_HEADS * GQA_DIM
SSD_W = SSD_HEADS * SSD_HEADDIM
DIFF_W = DIFF_HEADS * DIFF_VDIM
D_MIX = DN_W + GQA_W + SSD_W + DIFF_W
DN_CONV_CH = 2 * DN_HEADS * DN_DK + DN_W
SSD_CONV_CH = SSD_W + 2 * SSD_GROUPS * SSD_STATE
ALPHA = (2 * DEPTH) ** 0.25

COL_DN_QKV = 0
COL_DN_Z = 1536
COL_GQA = 2048
COL_SSD_Z = 3072
COL_SSD_XBC = 3584
COL_DIFF = 4608
D_MAIN = 6144
LANE_DN_B, LANE_DN_A, LANE_SSD_DT = 0, 8, 16
LOG2E = 1.4426950408889634

VMEM_LIMIT = 56 * 1024 * 1024
LANES = 128


def _cp(n_grid):
    return pltpu.CompilerParams(dimension_semantics=("arbitrary",) * n_grid, vmem_limit_bytes=VMEM_LIMIT)


def _dot(a, b):
    return jnp.dot(a, b, preferred_element_type=f32)


def _dot_nt(a, b):
    return lax.dot_general(a, b, (((1,), (1,)), ((), ())), preferred_element_type=f32)


def _dot_tn(a, b):
    return lax.dot_general(a, b, (((0,), (0,)), ((), ())), preferred_element_type=f32)


def _split_bf16(x, parts):
    out = []
    for _ in range(parts - 1):
        hi = x.astype(bf16)
        out.append(hi)
        x = x - hi.astype(f32)
    out.append(x.astype(bf16))
    return out


def _dot_sel_l(sel16, x, parts):
    n = x.shape[1]
    y = _dot(sel16, jnp.concatenate(_split_bf16(x, parts), axis=1))
    return sum(y[:, k * n:(k + 1) * n] for k in range(parts))


def _dot_sel_r(x, sel16, parts):
    m = x.shape[0]
    y = _dot(jnp.concatenate(_split_bf16(x, parts), axis=0), sel16)
    return sum(y[k * m:(k + 1) * m, :] for k in range(parts))


def _sigmoid(x):
    return 1.0 / (1.0 + jnp.exp(-x))


def _silu(x):
    return x * _sigmoid(x)


def _softplus(x):
    return jnp.maximum(x, 0.0) + jnp.log(1.0 + jnp.exp(-jnp.abs(x)))


def _layernorm(h, g, b):
    mu = jnp.mean(h, axis=-1, keepdims=True)
    hc = h - mu
    var = jnp.mean(hc * hc, axis=-1, keepdims=True)
    return hc * lax.rsqrt(var + 1e-5) * g + b


def _rms(x, g):
    return x * lax.rsqrt(jnp.mean(x * x, axis=-1, keepdims=True) + 1e-6) * g


def _rope(x, cos, sin_signed, blk):
    lane = lax.broadcasted_iota(jnp.int32, x.shape, 1)
    first = (lane & blk) == 0
    partner = jnp.where(first, pltpu.roll(x, LANES - blk, 1), pltpu.roll(x, blk, 1))
    return x * cos + partner * sin_signed


def _mod_body(s_ref, w_ref, b_ref, o_ref):
    o_ref[0] = _dot(_silu(s_ref[...]), w_ref[0]) + b_ref[0]


def _modulation(cvec, w_mod, b_mod):
    n_layers, d, n = w_mod.shape
    rows = cvec.shape[0]
    tn = 1024
    return pl.pallas_call(
        _mod_body,
        grid=(n_layers, n // tn),
        in_specs=[pl.BlockSpec((rows, d), lambda l, j: (0, 0)),
                  pl.BlockSpec((1, d, tn), lambda l, j: (l, 0, j)),
                  pl.BlockSpec((1, 1, tn), lambda l, j: (l, 0, j))],
        out_specs=pl.BlockSpec((1, rows, tn), lambda l, j: (l, 0, j)),
        out_shape=jax.ShapeDtypeStruct((n_layers, rows, n), f32),
        compiler_params=_cp(2), name="modulation",
    )(cvec, w_mod, b_mod.reshape(n_layers, 1, n))


def _ffn_body(x_ref, mod_ref, wg_ref, wu_ref, wd_ref, g_ref, b_ref, o_ref, hm_ref, acc_ref, *, base, nf):
    f = pl.program_id(1)

    @pl.when(f == 0)
    def _():
        hm_ref[...] = (x_ref[...] * (1.0 + mod_ref[base + 1:base + 2, :]) + mod_ref[base:base + 1, :]).astype(bf16)
        acc_ref[...] = jnp.zeros_like(acc_ref)

    hm = hm_ref[...]
    g = _dot(hm, wg_ref[...])
    u = _dot(hm, wu_ref[...])
    acc_ref[...] += _dot((_silu(g) * u).astype(bf16), wd_ref[...])

    @pl.when(f == nf - 1)
    def _():
        h = ALPHA * x_ref[...] + (0.5 * mod_ref[base + 2:base + 3, :]) * acc_ref[...]
        o_ref[...] = _layernorm(h, g_ref[...], b_ref[...])


def _row_tile(rows_per_mod):
    tm = min(512, rows_per_mod)
    assert rows_per_mod % tm == 0
    return tm


def _ffn(x, mods, mod_row0, rows_per_mod, base, wg, wu, wd, l, j, ln_g, ln_b, k):
    m, d = x.shape
    tm = _row_tile(rows_per_mod)
    tf = 512
    nf = D_FF // tf
    tpm = rows_per_mod // tm
    return pl.pallas_call(
        functools.partial(_ffn_body, base=base, nf=nf),
        grid=(m // tm, nf),
        in_specs=[pl.BlockSpec((tm, d), lambda i, f: (i, 0)),
                  pl.BlockSpec((None, None, N_MOD, d), lambda i, f: (l, mod_row0 + i // tpm, 0, 0)),
                  pl.BlockSpec((None, None, d, tf), lambda i, f: (l, j, 0, f)),
                  pl.BlockSpec((None, None, d, tf), lambda i, f: (l, j, 0, f)),
                  pl.BlockSpec((None, None, tf, d), lambda i, f: (l, j, f, 0)),
                  pl.BlockSpec((None, None, 1, d), lambda i, f: (l, k, 0, 0)),
                  pl.BlockSpec((None, None, 1, d), lambda i, f: (l, k, 0, 0))],
        out_specs=pl.BlockSpec((tm, d), lambda i, f: (i, 0)),
        out_shape=jax.ShapeDtypeStruct((m, d), f32),
        scratch_shapes=[pltpu.VMEM((tm, d), bf16), pltpu.VMEM((tm, d), f32)],
        compiler_params=_cp(2), name="half_ffn",
    )(x, mods, wg, wu, wd, ln_g, ln_b)


def _inproj_body(x_ref, mod_ref, w_ref, ws_ref, o_ref, os_ref, hm_ref):
    @pl.when(pl.program_id(1) == 0)
    def _():
        hm_ref[...] = (x_ref[...] * (1.0 + mod_ref[4:5, :]) + mod_ref[3:4, :]).astype(bf16)
        os_ref[...] = _dot(hm_ref[...], ws_ref[...])

    o_ref[...] = _dot(hm_ref[...], w_ref[...]).astype(o_ref.dtype)


def _inproj(x, mods, mod_row0, rows_per_mod, w, w_small, l):
    m, d = x.shape
    tm = 1024 if rows_per_mod % 1024 == 0 else _row_tile(rows_per_mod)
    tn = 1024
    tpm = rows_per_mod // tm
    return pl.pallas_call(
        _inproj_body,
        grid=(m // tm, D_MAIN // tn),
        in_specs=[pl.BlockSpec((tm, d), lambda i, n: (i, 0)),
                  pl.BlockSpec((None, None, N_MOD, d), lambda i, n: (l, mod_row0 + i // tpm, 0, 0)),
                  pl.BlockSpec((None, d, tn), lambda i, n: (l, 0, n)),
                  pl.BlockSpec((None, d, LANES), lambda i, n: (l, 0, 0))],
        out_specs=(pl.BlockSpec((tm, tn), lambda i, n: (i, n)), pl.BlockSpec((tm, LANES), lambda i, n: (i, 0))),
        out_shape=(jax.ShapeDtypeStruct((m, D_MAIN), bf16), jax.ShapeDtypeStruct((m, LANES), f32)),
        scratch_shapes=[pltpu.VMEM((tm, d), bf16)],
        compiler_params=_cp(2), name="in_proj",
    )(x, mods, w, w_small)


def _outproj_body(x_ref, a_ref, b_ref, c_ref, d_ref, w_ref, mod_ref, g_ref, be_ref, o_ref):
    y = _dot(a_ref[...], w_ref[0:DN_W, :])
    y += _dot(b_ref[...], w_ref[DN_W:DN_W + GQA_W, :])
    y += _dot(c_ref[...], w_ref[DN_W + GQA_W:DN_W + GQA_W + SSD_W, :])
    y += _dot(d_ref[...], w_ref[DN_W + GQA_W + SSD_W:D_MIX, :])
    h = ALPHA * x_ref[...] + mod_ref[5:6, :] * y
    o_ref[...] = _layernorm(h, g_ref[...], be_ref[...])


def _outproj(x, outs, w, mods, mod_row0, rows_per_mod, l, ln_g, ln_b):
    m, d = x.shape
    tm = _row_tile(rows_per_mod)
    tpm = rows_per_mod // tm
    mix_specs = [pl.BlockSpec((tm, o.shape[1]), lambda i: (i, 0)) for o in outs]
    return pl.pallas_call(
        _outproj_body,
        grid=(m // tm,),
        in_specs=[pl.BlockSpec((tm, d), lambda i: (i, 0))] + mix_specs + [
            pl.BlockSpec((None, D_MIX, d), lambda i: (l, 0, 0)),
            pl.BlockSpec((None, None, N_MOD, d), lambda i: (l, mod_row0 + i // tpm, 0, 0)),
            pl.BlockSpec((None, None, 1, d), lambda i: (l, 1, 0, 0)),
            pl.BlockSpec((None, None, 1, d), lambda i: (l, 1, 0, 0))],
        out_specs=pl.BlockSpec((tm, d), lambda i: (i, 0)),
        out_shape=jax.ShapeDtypeStruct((m, d), f32),
        compiler_params=_cp(1), name="out_proj",
    )(x, *outs, w, mods, ln_g, ln_b)


CONV_ROWS = 128
CONV_PAD = 8


def _conv_body(x_ref, w_ref, b_ref, o_ref, pad_ref, *, t, n_q, n_l2):
    cb = pl.program_id(1)
    c = x_ref.shape[1]
    pad_ref[0:CONV_PAD, :] = jnp.zeros((CONV_PAD, c), f32)
    pad_ref[CONV_PAD + t:2 * CONV_PAD + t, :] = jnp.zeros((CONV_PAD, c), f32)
    pad_ref[CONV_PAD:CONV_PAD + t, :] = x_ref[...].astype(f32)
    half = SHORT_CONV // 2
    rows = min(CONV_ROWS, t)
    for c0 in range(0, c, LANES):
        w = w_ref[:, c0:c0 + LANES]
        bias = b_ref[:, c0:c0 + LANES]
        for r0 in range(0, t, rows):
            acc = bias + pad_ref[pl.ds(CONV_PAD - half + r0, rows), c0:c0 + LANES] * w[0:1, :]
            for j in range(1, SHORT_CONV):
                acc += pad_ref[pl.ds(CONV_PAD - half + r0 + j, rows), c0:c0 + LANES] * w[j:j + 1, :]
            y = _silu(acc)
            if n_l2:
                r = lax.rsqrt(jnp.sum(y * y, axis=-1, keepdims=True) + 1e-6)
                y = y * jnp.where(cb < n_q, r * DN_DK ** -0.5, jnp.where(cb < n_l2, r, 1.0))
            o_ref[r0:r0 + rows, c0:c0 + LANES] = y.astype(o_ref.dtype)


def _conv(p, t, col0, n_ch, w, b, n_q, n_l2):
    m = p.shape[0]
    cw = 512
    cb0 = col0 // cw
    return pl.pallas_call(
        functools.partial(_conv_body, t=t, n_q=n_q, n_l2=n_l2),
        grid=(m // t, n_ch // cw),
        in_specs=[pl.BlockSpec((t, cw), lambda i, j: (i, cb0 + j)),
                  pl.BlockSpec((SHORT_CONV, cw), lambda i, j: (0, j)),
                  pl.BlockSpec((1, cw), lambda i, j: (0, j))],
        out_specs=pl.BlockSpec((t, cw), lambda i, j: (i, j)),
        out_shape=jax.ShapeDtypeStruct((m, n_ch), bf16),
        scratch_shapes=[pltpu.VMEM((t + 2 * CONV_PAD, cw), f32)],
        compiler_params=_cp(2), name="short_conv",
    )(p, w, b)


def _attend_transposed(k_s, vt_s, qt16, tk):
    n = 3 if tk % (3 * LANES) == 0 else 1
    ck = tk // n
    s_cur = _dot(k_s[0:ck, :], qt16)
    m = l = acc = None
    for c in range(n):
        s_next = _dot(k_s[(c + 1) * ck:(c + 2) * ck, :], qt16) if c + 1 < n else None
        mc = jnp.max(s_cur, axis=0, keepdims=True)
        m_new = mc if c == 0 else jnp.maximum(m, mc)
        p = jnp.exp2(s_cur - m_new)
        lc = jnp.sum(p, axis=0, keepdims=True)
        pv = _dot(vt_s[:, c * ck:(c + 1) * ck], p.astype(bf16))
        if c == 0:
            l, acc = lc, pv
        else:
            alpha = jnp.exp2(m - m_new)
            l = alpha * l + lc
            acc = alpha * acc + pv
        m, s_cur = m_new, s_next
    return acc, l


def _gqa_body(*refs, tc, tl, rope_q):
    if tl:
        (q_ref, kc_ref, vc_ref, kl_ref, vl_ref, cosq_ref, sinq_ref, cosk_ref, sink_ref,
         qn_ref, kn_ref, o_ref, k_s, v_s) = refs
    else:
        q_ref, kc_ref, vc_ref, qn_ref, kn_ref, o_ref, k_s, v_s = refs

    @pl.when(pl.program_id(2) == 0)
    def _():
        k_s[0:tc, :] = _rms(kc_ref[...].astype(f32), kn_ref[...]).astype(bf16)
        v_s[:, 0:tc] = vc_ref[...].astype(f32).T.astype(bf16)
        if tl:
            rb = min(512, tl)
            for r0 in range(0, tl, rb):
                kk = _rms(kl_ref[r0:r0 + rb, :].astype(f32), kn_ref[...])
                kk = _rope(kk, cosk_ref[r0:r0 + rb, :], sink_ref[r0:r0 + rb, :], GQA_DIM // 4)
                k_s[tc + r0:tc + r0 + rb, :] = kk.astype(bf16)
                v_s[:, tc + r0:tc + r0 + rb] = vl_ref[r0:r0 + rb, :].astype(f32).T.astype(bf16)

    tq = q_ref.shape[0]
    rep = GQA_HEADS // GQA_KV_HEADS
    qts = []
    for r in range(rep):
        q = _rms(q_ref[:, r * GQA_DIM:(r + 1) * GQA_DIM].astype(f32), qn_ref[...])
        if rope_q:
            q = _rope(q, cosq_ref[...], sinq_ref[...], GQA_DIM // 4)
        qts.append((q * (GQA_DIM ** -0.5 * LOG2E)).T)
    acc, l = _attend_transposed(k_s, v_s, jnp.concatenate(qts, axis=1).astype(bf16), tc + tl)
    o_t = acc * (1.0 / l)
    for r in range(rep):
        o_ref[:, r * GQA_DIM:(r + 1) * GQA_DIM] = o_t[:, r * tq:(r + 1) * tq].T.astype(o_ref.dtype)


def _gqa(pq, tq_seq, pc, tc, pl_, tl, rope, qn, kn):
    m = pq.shape[0]
    nb = m // tq_seq
    tq = min(512, tq_seq)
    rep = GQA_HEADS // GQA_KV_HEADS
    qw = rep * GQA_DIM
    qb0 = COL_GQA // qw
    kb0 = (COL_GQA + GQA_HEADS * GQA_DIM) // GQA_DIM
    vb0 = kb0 + GQA_KV_HEADS
    kv_spec = lambda t, b0: pl.BlockSpec((t, GQA_DIM), lambda b, g, i: (b, b0 + g))
    vec = pl.BlockSpec((1, GQA_DIM), lambda b, g, i: (0, 0))
    nq = tq_seq // tq
    in_specs = [pl.BlockSpec((tq, qw), lambda b, g, i: (b * nq + i, qb0 + g)), kv_spec(tc, kb0), kv_spec(tc, vb0)]
    args = [pq, pc, pc]
    if tl:
        cos, sin = rope
        in_specs += [kv_spec(tl, kb0), kv_spec(tl, vb0),
                     pl.BlockSpec((tq, GQA_DIM), lambda b, g, i: (i, 0)), pl.BlockSpec((tq, GQA_DIM), lambda b, g, i: (i, 0)),
                     pl.BlockSpec((tl, GQA_DIM), lambda b, g, i: (0, 0)), pl.BlockSpec((tl, GQA_DIM), lambda b, g, i: (0, 0))]
        args += [pl_, pl_, cos, sin, cos, sin]
    in_specs += [vec, vec]
    args += [qn, kn]
    return pl.pallas_call(
        functools.partial(_gqa_body, tc=tc, tl=tl, rope_q=bool(tl)),
        grid=(nb, GQA_KV_HEADS, nq),
        in_specs=in_specs,
        out_specs=pl.BlockSpec((tq, qw), lambda b, g, i: (b * nq + i, g)),
        out_shape=jax.ShapeDtypeStruct((m, GQA_W), bf16),
        scratch_shapes=[pltpu.VMEM((tc + tl, GQA_DIM), bf16), pltpu.VMEM((GQA_DIM, tc + tl), bf16)],
        compiler_params=_cp(3), name="gqa_attention",
    )(*args)


def _diff_body(*refs, tc, tl, lam_init):
    if tl:
        (q_ref, kc_ref, vc_ref, kl_ref, vl_ref, cosq_ref, sinq_ref, cosk_ref, sink_ref,
         lam_ref, ng_ref, o_ref, k_s, v_s) = refs
    else:
        q_ref, kc_ref, vc_ref, lam_ref, ng_ref, o_ref, k_s, v_s = refs

    @pl.when(pl.program_id(2) == 0)
    def _():
        k_s[0:tc, :] = kc_ref[...]
        v_s[:, 0:tc] = vc_ref[...].astype(f32).T.astype(bf16)
        if tl:
            rb = min(512, tl)
            for r0 in range(0, tl, rb):
                kk = _rope(kl_ref[r0:r0 + rb, :].astype(f32), cosk_ref[r0:r0 + rb, :], sink_ref[r0:r0 + rb, :],
                           DIFF_DIM // 4)
                k_s[tc + r0:tc + r0 + rb, :] = kk.astype(bf16)
                v_s[:, tc + r0:tc + r0 + rb] = vl_ref[r0:r0 + rb, :].astype(f32).T.astype(bf16)

    lv = lam_ref[...]
    lam = (jnp.exp(jnp.sum(lv[0:1, :] * lv[1:2, :], axis=-1, keepdims=True))
           - jnp.exp(jnp.sum(lv[2:3, :] * lv[3:4, :], axis=-1, keepdims=True)) + lam_init)
    q = q_ref[...].astype(f32)
    if tl:
        q = _rope(q, cosq_ref[...], sinq_ref[...], DIFF_DIM // 4)
    tq = q.shape[0]
    qt = (q * (DIFF_DIM ** -0.5 * LOG2E)).T
    row = lax.broadcasted_iota(jnp.int32, qt.shape, 0)
    qt16 = jnp.concatenate([jnp.where(row < DIFF_DIM, qt, 0.0), jnp.where(row >= DIFF_DIM, qt, 0.0)],
                           axis=1).astype(bf16)
    acc, l = _attend_transposed(k_s, v_s, qt16, tc + tl)
    o_t = acc * (1.0 / l)
    o = (o_t[:, 0:tq] - lam * o_t[:, tq:2 * tq]).T
    o_ref[...] = (_rms(o, ng_ref[...]) * (1.0 - lam_init)).astype(o_ref.dtype)


def _diff(pq, tq_seq, pc, tc, pl_, tl, rope, lam_vecs, ng, lam_init):
    m = pq.shape[0]
    nb = m // tq_seq
    tq = min(512, tq_seq)
    nq = tq_seq // tq
    w = DIFF_VDIM
    qb0 = COL_DIFF // w
    kb0 = qb0 + DIFF_HEADS
    vb0 = kb0 + DIFF_HEADS
    kv_spec = lambda t, b0: pl.BlockSpec((t, w), lambda b, h, i: (b, b0 + h))
    in_specs = [pl.BlockSpec((tq, w), lambda b, h, i: (b * nq + i, qb0 + h)), kv_spec(tc, kb0), kv_spec(tc, vb0)]
    args = [pq, pc, pc]
    if tl:
        cos, sin = rope
        in_specs += [kv_spec(tl, kb0), kv_spec(tl, vb0),
                     pl.BlockSpec((tq, w), lambda b, h, i: (i, 0)), pl.BlockSpec((tq, w), lambda b, h, i: (i, 0)),
                     pl.BlockSpec((tl, w), lambda b, h, i: (0, 0)), pl.BlockSpec((tl, w), lambda b, h, i: (0, 0))]
        args += [pl_, pl_, cos, sin, cos, sin]
    in_specs += [pl.BlockSpec((4, DIFF_DIM), lambda b, h, i: (0, 0)), pl.BlockSpec((1, w), lambda b, h, i: (0, 0))]
    args += [lam_vecs, ng]
    return pl.pallas_call(
        functools.partial(_diff_body, tc=tc, tl=tl, lam_init=lam_init),
        grid=(nb, DIFF_HEADS, nq),
        in_specs=in_specs,
        out_specs=pl.BlockSpec((tq, w), lambda b, h, i: (b * nq + i, h)),
        out_shape=jax.ShapeDtypeStruct((m, DIFF_W), bf16),
        scratch_shapes=[pltpu.VMEM((tc + tl, w), bf16), pltpu.VMEM((w, tc + tl), bf16)],
        compiler_params=_cp(3), name="diff_attention",
    )(*args)


def _tri_masks(n):
    row = lax.broadcasted_iota(jnp.int32, (n, n), 0)
    col = lax.broadcasted_iota(jnp.int32, (n, n), 1)
    return row >= col, row <= col, row > col, row < col


def _dn_body(*refs, tc, tl, need_ctx):
    (qc_ref, kc_ref, vc_ref, zc_ref, gc_ref, gtc_ref,
     ql_ref, kl_ref, vl_ref, zl_ref, gl_ref, gtl_ref, cp_ref, rp_ref, ng_ref) = refs[:15]
    if need_ctx:
        oc_ref, ol_ref, o_s, s_s = refs[15:]
    else:
        ol_ref, o_s, s_s = refs[15:]
        oc_ref = None
    c = DN_CHUNK
    ge, le, gt, lt = _tri_masks(c)
    incl = (ge, le)
    strict = (gt, lt)
    ge16 = jnp.where(ge, 1.0, 0.0).astype(bf16)
    le16 = jnp.where(le, 1.0, 0.0).astype(bf16)
    tri_col = (ge16, le16)
    tri_row = (le16, ge16)
    eye = (lax.broadcasted_iota(jnp.int32, (c, c), 0) == lax.broadcasted_iota(jnp.int32, (c, c), 1)).astype(f32)
    neg_a_row = -jnp.exp(cp_ref[0:1, :])
    dtb_row = cp_ref[1:2, :]
    neg_a_col = -jnp.exp(rp_ref[:, 0:1])
    dtb_col = rp_ref[:, 1:2]
    s_s[...] = jnp.zeros_like(s_s)

    def run(q_ref, k_ref, v_ref, g_ref, gt_ref, n_chunks, base):
        def step(i, carry):
            dirs = []
            for d in range(2):
                ci = i if d == 0 else n_chunks - 1 - i
                r = pl.multiple_of(ci * c, c)
                graw = g_ref[pl.ds(r, c), :]
                dirs.append(dict(r=r, beta_all=_sigmoid(graw), gdec=neg_a_row * _softplus(graw + dtb_row),
                                 gdec_r=neg_a_col * _softplus(gt_ref[ci] + dtb_col)))
            for d, dd in enumerate(dirs):
                dd["cum"] = _dot_sel_l(tri_col[d], dd["gdec"], 3)
                dd["cum_r"] = _dot_sel_r(dd["gdec_r"], tri_row[d], 3)
                dd["total"] = jnp.sum(dd["gdec"], axis=0, keepdims=True)
            chains = []
            for d, dd in enumerate(dirs):
                for h in range(DN_HEADS):
                    lb = LANE_DN_B + d * DN_HEADS + h
                    la = LANE_DN_A + d * DN_HEADS + h
                    hs = slice(h * DN_DK, (h + 1) * DN_DK)
                    gcol = dd["cum"][:, la:la + 1]
                    ch = dict(d=d, hs=hs, r=dd["r"], si=d * DN_HEADS + h, beta=dd["beta_all"][:, lb:lb + 1], gcol=gcol,
                              tot=dd["total"][:, la:la + 1], eg=jnp.exp(gcol),
                              decay=jnp.where(incl[d], jnp.exp(jnp.where(incl[d], gcol - dd["cum_r"][la:la + 1, :], 0.0)), 0.0),
                              q16=q_ref[pl.ds(dd["r"], c), hs], k16=k_ref[pl.ds(dd["r"], c), hs],
                              vf=v_ref[pl.ds(dd["r"], c), hs].astype(f32))
                    ch["kf"] = ch["k16"].astype(f32)
                    chains.append(ch)
            for ch in chains:
                kq = _dot_nt(jnp.concatenate([ch["k16"], ch["q16"]], axis=0), ch["k16"])
                ch["kk"] = kq[0:c, :]
                ch["qk"] = (kq[c:2 * c, :] * ch["decay"]).astype(bf16)
            for ch in chains:
                ch["m"] = -jnp.where(strict[ch["d"]], ch["beta"] * ch["kk"] * ch["decay"], 0.0)
                ch["inv"] = eye + ch["m"]
            for _ in range(int(math.log2(c)) - 1):
                for ch in chains:
                    m16 = ch["m"].astype(bf16)
                    ch["m"] = _dot(m16, m16)
                for ch in chains:
                    ch["inv"] = _dot(ch["inv"].astype(bf16), (eye + ch["m"]).astype(bf16))
            for ch in chains:
                rhs = jnp.concatenate([(ch["vf"] * ch["beta"]).astype(bf16),
                                       (ch["kf"] * (ch["beta"] * ch["eg"])).astype(bf16)], axis=1)
                ch["uw"] = _dot(ch["inv"].astype(bf16), rhs)
            for ch in chains:
                ch["s"] = s_s[ch["si"]]
                lhs = jnp.concatenate([ch["uw"][:, DN_DV:DN_DV + DN_DK].astype(bf16),
                                       (ch["q16"].astype(f32) * ch["eg"]).astype(bf16)], axis=0)
                ch["ws"] = _dot(lhs, ch["s"].astype(bf16))
                ch["v16"] = (ch["uw"][:, 0:DN_DV] - ch["ws"][0:c, :]).astype(bf16)
            for ch in chains:
                ch["o"] = ch["ws"][c:2 * c, :] + _dot(ch["qk"], ch["v16"])
            for ch in chains:
                kd = (ch["kf"] * jnp.exp(ch["tot"] - ch["gcol"])).astype(bf16)
                s_s[ch["si"]] = ch["s"] * jnp.exp(ch["tot"]) + _dot_tn(kd, ch["v16"])
            for ch in chains:
                o_s[ch["d"], pl.ds(base + ch["r"], c), ch["hs"]] = ch["o"]
            return carry

        lax.fori_loop(0, n_chunks, step, 0)

    run(qc_ref, kc_ref, vc_ref, gc_ref, gtc_ref, tc // c, 0)
    run(ql_ref, kl_ref, vl_ref, gl_ref, gtl_ref, tl // c, tc)

    def finish(z_ref, out_ref, base, t):
        rb = min(256, t)
        for r0 in range(0, t, rb):
            for h in range(DN_HEADS):
                hs = slice(h * DN_DV, (h + 1) * DN_DV)
                o = o_s[0, base + r0:base + r0 + rb, hs] + o_s[1, base + r0:base + r0 + rb, hs]
                out_ref[r0:r0 + rb, hs] = (_rms(o, ng_ref[...]) * _silu(z_ref[r0:r0 + rb, hs].astype(f32))).astype(out_ref.dtype)

    finish(zl_ref, ol_ref, tc, tl)
    if need_ctx:
        finish(zc_ref, oc_ref, 0, tc)


def _deltanet(qkv_c, p_c, ps_c, gt_c, tc, qkv_l, p_l, ps_l, gt_l, tl, cparams, rparams, ng, need_ctx):
    nb = qkv_l.shape[0] // tl
    w = DN_W
    c = DN_CHUNK

    def seq_specs(t):
        return [pl.BlockSpec((t, w), lambda b: (b, 0)), pl.BlockSpec((t, w), lambda b: (b, 1)),
                pl.BlockSpec((t, w), lambda b: (b, 2)),
                pl.BlockSpec((t, w), lambda b: (b, COL_DN_Z // w)),
                pl.BlockSpec((t, LANES), lambda b: (b, 0)),
                pl.BlockSpec((None, t // c, 32, c), lambda b: (b, 0, 0, 0))]

    in_specs = seq_specs(tc) + seq_specs(tl) + [
        pl.BlockSpec((8, LANES), lambda b: (0, 0)), pl.BlockSpec((32, 2), lambda b: (0, 0)),
        pl.BlockSpec((1, DN_DV), lambda b: (0, 0))]
    out_l = jax.ShapeDtypeStruct((nb * tl, w), bf16)
    out_c = jax.ShapeDtypeStruct((nb * tc, w), bf16)
    spec_l = pl.BlockSpec((tl, w), lambda b: (b, 0))
    spec_c = pl.BlockSpec((tc, w), lambda b: (b, 0))
    res = pl.pallas_call(
        functools.partial(_dn_body, tc=tc, tl=tl, need_ctx=need_ctx),
        grid=(nb,),
        in_specs=in_specs,
        out_specs=(spec_c, spec_l) if need_ctx else spec_l,
        out_shape=(out_c, out_l) if need_ctx else out_l,
        scratch_shapes=[pltpu.VMEM((2, tc + tl, w), f32), pltpu.VMEM((2 * DN_HEADS, DN_DK, DN_DV), f32)],
        compiler_params=_cp(1), name="gated_deltanet",
    )(qkv_c, qkv_c, qkv_c, p_c, ps_c, gt_c, qkv_l, qkv_l, qkv_l, p_l, ps_l, gt_l, cparams, rparams, ng)
    return (res[1], res[0]) if need_ctx else (res, None)


def _ssd_body(*refs, tc, tl, need_ctx):
    (xc_ref, zc_ref, gc_ref, gtc_ref, xl_ref, zl_ref, gl_ref, gtl_ref,
     cp_ref, rp_ref, ex_ref, dsk_ref, ng_ref) = refs[:13]
    if need_ctx:
        oc_ref, ol_ref, y_s, s_s = refs[13:]
    else:
        ol_ref, y_s, s_s = refs[13:]
        oc_ref = None
    c = SSD_CHUNK
    ge, le, _, _ = _tri_masks(c)
    incl = (ge, le)
    ge16 = jnp.where(ge, 1.0, 0.0).astype(bf16)
    le16 = jnp.where(le, 1.0, 0.0).astype(bf16)
    tri_col = (ge16, le16)
    tri_row = (le16, ge16)
    neg_a_row = -jnp.exp(cp_ref[0:1, :])
    dtb_row = cp_ref[1:2, :]
    neg_a_col = -jnp.exp(rp_ref[:, 0:1])
    dtb_col = rp_ref[:, 1:2]
    lane = lax.broadcasted_iota(jnp.int32, (c, LANES), 1)
    lo = lane < SSD_HEADDIM
    hi = lane >= SSD_HEADDIM
    s_s[...] = jnp.zeros_like(s_s)
    hpg = SSD_HEADS // SSD_GROUPS
    gw = hpg * SSD_HEADDIM
    b0 = SSD_W
    c0 = SSD_W + SSD_GROUPS * SSD_STATE

    def run(x_ref, g_ref, gt_ref, n_chunks, base):
        def step(i, carry):
            dirs = []
            for d in range(2):
                ci = i if d == 0 else n_chunks - 1 - i
                r = pl.multiple_of(ci * c, c)
                dt = _softplus(g_ref[pl.ds(r, c), :] + dtb_row)
                dirs.append(dict(r=r, dt=dt, da=dt * neg_a_row, da_r=_softplus(gt_ref[ci] + dtb_col) * neg_a_col))
            for d, dd in enumerate(dirs):
                dd["cum"] = _dot_sel_l(tri_col[d], dd["da"], 3)
                dd["cum_r"] = _dot_sel_r(dd["da_r"], tri_row[d], 3)
                dd["total"] = jnp.sum(dd["da"], axis=0, keepdims=True)
            for d, dd in enumerate(dirs):
                stacked = jnp.concatenate([dd["dt"], dd["dt"] * jnp.exp(dd["total"] - dd["cum"]), jnp.exp(dd["cum"]),
                                           jnp.broadcast_to(jnp.exp(dd["total"]), (8, LANES))], axis=0)
                e_all = _dot_sel_r(stacked, ex_ref[d], 2)
                xs = x_ref[pl.ds(dd["r"], c), 0:SSD_W].astype(f32)
                dd["xdt"] = xs * e_all[0:c, :]
                dd["xdtw"] = (xs * e_all[c:2 * c, :]).astype(bf16)
                dd["cin_e"] = e_all[2 * c:3 * c, :]
                dd["cd_e"] = e_all[3 * c:3 * c + 1, :]
            groups = []
            for d, dd in enumerate(dirs):
                for g in range(SSD_GROUPS):
                    groups.append(dict(d=d, g=g, dd=dd, si=d * SSD_GROUPS + g, gs=slice(g * gw, (g + 1) * gw),
                                       bm=x_ref[pl.ds(dd["r"], c), b0 + g * SSD_STATE:b0 + (g + 1) * SSD_STATE],
                                       cm=x_ref[pl.ds(dd["r"], c), c0 + g * SSD_STATE:c0 + (g + 1) * SSD_STATE]))
            for gg in groups:
                gg["scores"] = _dot_nt(gg["cm"], gg["bm"])
            for gg in groups:
                gg["s"] = s_s[gg["si"]]
                gg["y_off"] = _dot(gg["cm"], gg["s"].astype(bf16)) * gg["dd"]["cin_e"][:, gg["gs"]]
            for gg in groups:
                s_s[gg["si"]] = gg["s"] * gg["dd"]["cd_e"][:, gg["gs"]] + _dot_tn(gg["bm"], gg["dd"]["xdtw"][:, gg["gs"]])
            items = []
            for gg in groups:
                d, g, dd = gg["d"], gg["g"], gg["dd"]
                for pair in range(hpg // 2):
                    ps = slice(g * gw + pair * LANES, g * gw + (pair + 1) * LANES)
                    xp = dd["xdt"][:, ps]
                    for e in range(2):
                        la = LANE_SSD_DT + d * SSD_HEADS + g * hpg + pair * 2 + e
                        seg = dd["cum"][:, la:la + 1] - dd["cum_r"][la:la + 1, :]
                        decay = jnp.where(incl[d], jnp.exp(jnp.where(incl[d], seg, 0.0)), 0.0)
                        items.append(dict(m=(gg["scores"] * decay).astype(bf16),
                                          xh=jnp.where(lo if e == 0 else hi, xp, 0.0).astype(bf16)))
                    gg.setdefault("pairs", []).append((ps, pair, len(items) - 2))
            for it in items:
                it["y"] = _dot(it["m"], it["xh"])
            for gg in groups:
                for ps, pair, k in gg["pairs"]:
                    yp = gg["y_off"][:, pair * LANES:(pair + 1) * LANES] + items[k]["y"] + items[k + 1]["y"]
                    y_s[gg["d"], pl.ds(base + gg["dd"]["r"], c), ps] = yp
            return carry

        lax.fori_loop(0, n_chunks, step, 0)

    run(xc_ref, gc_ref, gtc_ref, tc // c, 0)
    run(xl_ref, gl_ref, gtl_ref, tl // c, tc)

    def finish(x_ref, z_ref, out_ref, base, t):
        rb = min(256, t)
        gcols = SSD_W // SSD_GROUPS
        for r0 in range(0, t, rb):
            for g in range(SSD_GROUPS):
                gs = slice(g * gcols, (g + 1) * gcols)
                y = y_s[0, base + r0:base + r0 + rb, gs] + y_s[1, base + r0:base + r0 + rb, gs]
                y = (y + dsk_ref[:, gs] * x_ref[r0:r0 + rb, gs].astype(f32)) * _silu(z_ref[r0:r0 + rb, gs].astype(f32))
                out_ref[r0:r0 + rb, gs] = _rms(y, ng_ref[:, gs]).astype(out_ref.dtype)

    finish(xl_ref, zl_ref, ol_ref, tc, tl)
    if need_ctx:
        finish(xc_ref, zc_ref, oc_ref, 0, tc)


def _ssd(xbc_c, p_c, ps_c, gt_c, tc, xbc_l, p_l, ps_l, gt_l, tl, cparams, rparams, expand, d_skip, ng, need_ctx):
    nb = xbc_l.shape[0] // tl
    w = SSD_W
    c = SSD_CHUNK

    def seq_specs(t):
        return [pl.BlockSpec((t, SSD_CONV_CH), lambda b: (b, 0)),
                pl.BlockSpec((t, w), lambda b: (b, COL_SSD_Z // w)),
                pl.BlockSpec((t, LANES), lambda b: (b, 0)),
                pl.BlockSpec((None, t // c, 32, c), lambda b: (b, 0, 0, 0))]

    in_specs = seq_specs(tc) + seq_specs(tl) + [
        pl.BlockSpec((8, LANES), lambda b: (0, 0)), pl.BlockSpec((32, 2), lambda b: (0, 0)),
        pl.BlockSpec((2, LANES, w), lambda b: (0, 0, 0)),
        pl.BlockSpec((1, w), lambda b: (0, 0)), pl.BlockSpec((1, w), lambda b: (0, 0))]
    out_l = jax.ShapeDtypeStruct((nb * tl, w), bf16)
    out_c = jax.ShapeDtypeStruct((nb * tc, w), bf16)
    spec_l = pl.BlockSpec((tl, w), lambda b: (b, 0))
    spec_c = pl.BlockSpec((tc, w), lambda b: (b, 0))
    res = pl.pallas_call(
        functools.partial(_ssd_body, tc=tc, tl=tl, need_ctx=need_ctx),
        grid=(nb,),
        in_specs=in_specs,
        out_specs=(spec_c, spec_l) if need_ctx else spec_l,
        out_shape=(out_c, out_l) if need_ctx else out_l,
        scratch_shapes=[pltpu.VMEM((2, tc + tl, w), f32),
                        pltpu.VMEM((2 * SSD_GROUPS, SSD_STATE, w // SSD_GROUPS), f32)],
        compiler_params=_cp(1), name="ssd_scan",
    )(xbc_c, p_c, ps_c, gt_c, xbc_l, p_l, ps_l, gt_l, cparams, rparams, expand, d_skip, ng)
    return (res[1], res[0]) if need_ctx else (res, None)


def _rope_tables(n_tok, head_dim, reps):
    rows = n_tok // GRID_W
    row = jnp.broadcast_to(jnp.arange(rows)[:, None], (rows, GRID_W)).reshape(-1).astype(f32)
    col = jnp.broadcast_to(jnp.arange(GRID_W)[None, :], (rows, GRID_W)).reshape(-1).astype(f32)
    half = head_dim // 2
    inv = ROPE_THETA ** (-jnp.arange(0, half, 2, dtype=f32) / half)
    ar = row[:, None] * inv
    ac = col[:, None] * inv
    cos = jnp.concatenate([jnp.cos(ar), jnp.cos(ar), jnp.cos(ac), jnp.cos(ac)], axis=-1)
    sin = jnp.concatenate([-jnp.sin(ar), jnp.sin(ar), -jnp.sin(ac), jnp.sin(ac)], axis=-1)
    return jnp.tile(cos, (1, reps)), jnp.tile(sin, (1, reps))


W_IN_A_SMALL = DN_CONV_CH + DN_W
W_IN_B0 = W_IN_A_SMALL + 4 * DN_HEADS
W_IN_C_SMALL = W_IN_B0 + (GQA_HEADS + 2 * GQA_KV_HEADS) * GQA_DIM + SSD_W + SSD_CONV_CH
W_IN_D0 = W_IN_C_SMALL + 2 * SSD_HEADS
D_IN = W_IN_D0 + 4 * DIFF_HEADS * DIFF_DIM + DIFF_W


def _pack_body(w_ref, o_ref, s_ref):
    n1 = W_IN_A_SMALL
    n2 = n1 + (W_IN_C_SMALL - W_IN_B0)
    o_ref[:, 0:n1] = w_ref[:, 0:n1].astype(bf16)
    o_ref[:, n1:n2] = w_ref[:, W_IN_B0:W_IN_C_SMALL].astype(bf16)
    o_ref[:, n2:D_MAIN] = w_ref[:, W_IN_D0:D_IN].astype(bf16)
    pad = jnp.zeros((w_ref.shape[0], LANES - 32), f32)
    s_ref[...] = jnp.concatenate([w_ref[:, W_IN_A_SMALL:W_IN_B0], w_ref[:, W_IN_C_SMALL:W_IN_D0], pad], axis=1).astype(bf16)


def _pack_w_in(w_in):
    n_layers, d, d_in = w_in.shape
    assert d_in == D_IN
    tr = 256
    return pl.pallas_call(
        _pack_body,
        grid=(n_layers, d // tr),
        in_specs=[pl.BlockSpec((None, tr, D_IN), lambda l, i: (l, i, 0))],
        out_specs=(pl.BlockSpec((None, tr, D_MAIN), lambda l, i: (l, i, 0)),
                   pl.BlockSpec((None, tr, LANES), lambda l, i: (l, i, 0))),
        out_shape=(jax.ShapeDtypeStruct((n_layers, d, D_MAIN), bf16), jax.ShapeDtypeStruct((n_layers, d, LANES), bf16)),
        compiler_params=_cp(2), name="pack_w_in",
    )(w_in)


def _gates_rowlayout(ps, t, chunk):
    g = ps[:, 0:32]
    nb = g.shape[0] // t
    return g.reshape(nb, t // chunk, chunk, 32).transpose(0, 1, 3, 2)


def _lane_params(a_log, dt_bias, lane0):
    n = a_log.size
    out = jnp.zeros((8, LANES), f32)
    out = out.at[0, lane0:lane0 + n].set(a_log.reshape(-1))
    return out.at[1, lane0:lane0 + n].set(dt_bias.reshape(-1))


def _row_params(a_log, dt_bias, row0):
    n = a_log.size
    out = jnp.zeros((32, 2), f32)
    out = out.at[row0:row0 + n, 0].set(a_log.reshape(-1))
    return out.at[row0:row0 + n, 1].set(dt_bias.reshape(-1))


def _ssd_expand():
    lane = jnp.arange(LANES)[:, None]
    colh = (jnp.arange(SSD_W) // SSD_HEADDIM)[None, :]
    return jnp.stack([(lane == LANE_SSD_DT + d * SSD_HEADS + colh).astype(bf16) for d in range(2)])


def kernel(x, c, ctx, c_ctx, w_mod, b_mod, ln_g, ln_b, ffn_wg, ffn_wu, ffn_wd, w_in, w_out, dn_conv, dn_a_log, dn_dt_bias, dn_norm_g, gqa_q_norm, gqa_k_norm, ssd_conv, ssd_conv_b, ssd_a_log, ssd_dt_bias, ssd_d, ssd_norm_g, diff_lambda, diff_norm_g):
    nb, tl, d = x.shape
    tc = ctx.shape[1]
    n_layers = w_mod.shape[0]
    xl = x.reshape(nb * tl, d)
    xc = ctx.reshape(nb * tc, d)

    rows = -(-(nb + 1) // 8) * 8
    cvec = jnp.zeros((rows, d), f32).at[:nb].set(c).at[nb].set(c_ctx)
    mods = _modulation(cvec, w_mod, b_mod).reshape(n_layers, rows, N_MOD, d)

    wg16, wu16, wd16 = ffn_wg.astype(bf16), ffn_wu.astype(bf16), ffn_wd.astype(bf16)
    win16, win_small16 = _pack_w_in(w_in)
    wout16 = w_out.astype(bf16)
    lng = ln_g.reshape(n_layers, 3, 1, d)
    lnb = ln_b.reshape(n_layers, 3, 1, d)
    rope_gqa = _rope_tables(tl, GQA_DIM, 1)
    rope_diff = _rope_tables(tl, DIFF_DIM, 2)
    ssd_ex = _ssd_expand()
    zero_bias_dn = jnp.zeros((1, DN_CONV_CH), f32)

    lat = dict(mod_row0=0, rows_per_mod=tl)
    cx = dict(mod_row0=nb, rows_per_mod=nb * tc)

    for l in range(n_layers):
        need_ctx = l < n_layers - 1
        lam_init = 0.8 - 0.6 * math.exp(-0.3 * l)
        ffn = lambda h, grp, base, j, k: _ffn(h, mods, grp["mod_row0"], grp["rows_per_mod"], base,
                                              wg16, wu16, wd16, l, j, lng, lnb, k)
        xl = ffn(xl, lat, 0, 0, 0)
        xc = ffn(xc, cx, 0, 0, 0)

        p_l, ps_l = _inproj(xl, mods, 0, tl, win16, win_small16, l)
        p_c, ps_c = _inproj(xc, mods, nb, nb * tc, win16, win_small16, l)

        dn_qkv_l = _conv(p_l, tl, COL_DN_QKV, DN_CONV_CH, dn_conv[l], zero_bias_dn, 1, 2)
        dn_qkv_c = _conv(p_c, tc, COL_DN_QKV, DN_CONV_CH, dn_conv[l], zero_bias_dn, 1, 2)
        a_l, a_c = _deltanet(dn_qkv_c, p_c, ps_c, _gates_rowlayout(ps_c, tc, DN_CHUNK), tc,
                             dn_qkv_l, p_l, ps_l, _gates_rowlayout(ps_l, tl, DN_CHUNK), tl,
                             _lane_params(dn_a_log[l], dn_dt_bias[l], LANE_DN_A),
                             _row_params(dn_a_log[l], dn_dt_bias[l], LANE_DN_A),
                             dn_norm_g[l].reshape(1, DN_DV), need_ctx)
        qn = gqa_q_norm[l].reshape(1, GQA_DIM)
        kn = gqa_k_norm[l].reshape(1, GQA_DIM)
        b_l = _gqa(p_l, tl, p_c, tc, p_l, tl, rope_gqa, qn, kn)
        b_c = _gqa(p_c, tc, p_c, tc, None, 0, None, qn, kn) if need_ctx else None
        ssd_xbc_l = _conv(p_l, tl, COL_SSD_XBC, SSD_CONV_CH, ssd_conv[l], ssd_conv_b[l].reshape(1, -1), 0, 0)
        ssd_xbc_c = _conv(p_c, tc, COL_SSD_XBC, SSD_CONV_CH, ssd_conv[l], ssd_conv_b[l].reshape(1, -1), 0, 0)
        c_l, c_c = _ssd(ssd_xbc_c, p_c, ps_c, _gates_rowlayout(ps_c, tc, SSD_CHUNK), tc,
                        ssd_xbc_l, p_l, ps_l, _gates_rowlayout(ps_l, tl, SSD_CHUNK), tl,
                        _lane_params(ssd_a_log[l], ssd_dt_bias[l], LANE_SSD_DT),
                        _row_params(ssd_a_log[l], ssd_dt_bias[l], LANE_SSD_DT),
                        ssd_ex, jnp.repeat(ssd_d[l], SSD_HEADDIM).reshape(1, SSD_W),
                        ssd_norm_g[l].reshape(1, SSD_W), need_ctx)
        ng = diff_norm_g[l].reshape(1, DIFF_VDIM)
        d_l = _diff(p_l, tl, p_c, tc, p_l, tl, rope_diff, diff_lambda[l], ng, lam_init)
        d_c = _diff(p_c, tc, p_c, tc, None, 0, None, diff_lambda[l], ng, lam_init) if need_ctx else None

        xl = _outproj(xl, (a_l, b_l, c_l, d_l), wout16, mods, 0, tl, l, lng, lnb)
        xl = ffn(xl, lat, 6, 1, 2)
        if need_ctx:
            xc = _outproj(xc, (a_c, b_c, c_c, d_c), wout16, mods, nb, nb * tc, l, lng, lnb)
            xc = ffn(xc, cx, 6, 1, 2)
    return xl.reshape(nb, tl, d)
```

```python
import functools
import math

import jax
import jax.numpy as jnp
from jax import lax
from jax.experimental import pallas as pl
from jax.experimental.pallas import tpu as pltpu

f32 = jnp.float32
bf16 = jnp.bfloat16

D_MODEL = 2048
DEPTH = 2
GRID_W = 64
D_FF = 5632
SHORT_CONV = 5
ROPE_THETA = 10000.0
N_MOD = 9
DN_HEADS, DN_DK, DN_DV, DN_CHUNK = 4, 128, 128, 64
GQA_HEADS, GQA_KV_HEADS, GQA_DIM = 4, 2, 128
SSD_HEADS, SSD_HEADDIM, SSD_GROUPS, SSD_STATE, SSD_CHUNK = 8, 64, 2, 128, 128
DIFF_HEADS, DIFF_DIM, DIFF_VDIM = 4, 64, 128
DN_W = DN_HEADS * DN_DV
GQA_W = GQA_HEADS * GQA_DIM
SSD_W = SSD_HEADS * SSD_HEADDIM
DIFF_W = DIFF_HEADS * DIFF_VDIM
D_MIX = DN_W + GQA_W + SSD_W + DIFF_W
DN_CONV_CH = 2 * DN_HEADS * DN_DK + DN_W
SSD_CONV_CH = SSD_W + 2 * SSD_GROUPS * SSD_STATE
ALPHA = (2 * DEPTH) ** 0.25

COL_DN_QKV = 0
COL_DN_Z = 1536
COL_GQA = 2048
COL_SSD_Z = 3072
COL_SSD_XBC = 3584
COL_DIFF = 4608
D_MAIN = 6144
LANE_DN_B, LANE_DN_A, LANE_SSD_DT = 0, 8, 16
LOG2E = 1.4426950408889634

VMEM_LIMIT = 56 * 1024 * 1024
LANES = 128


def _cp(n_grid):
    return pltpu.CompilerParams(dimension_semantics=("arbitrary",) * n_grid, vmem_limit_bytes=VMEM_LIMIT)


def _dot(a, b):
    return jnp.dot(a, b, preferred_element_type=f32)


def _dot_nt(a, b):
    return lax.dot_general(a, b, (((1,), (1,)), ((), ())), preferred_element_type=f32)


def _dot_tn(a, b):
    return lax.dot_general(a, b, (((0,), (0,)), ((), ())), preferred_element_type=f32)


def _split_bf16(x, parts):
    out = []
    for _ in range(parts - 1):
        hi = x.astype(bf16)
        out.append(hi)
        x = x - hi.astype(f32)
    out.append(x.astype(bf16))
    return out


def _dot_sel_l(sel16, x, parts):
    n = x.shape[1]
    y = _dot(sel16, jnp.concatenate(_split_bf16(x, parts), axis=1))
    return sum(y[:, k * n:(k + 1) * n] for k in range(parts))


def _dot_sel_r(x, sel16, parts):
    m = x.shape[0]
    y = _dot(jnp.concatenate(_split_bf16(x, parts), axis=0), sel16)
    return sum(y[k * m:(k + 1) * m, :] for k in range(parts))


def _sigmoid(x):
    return 1.0 / (1.0 + jnp.exp(-x))


def _silu(x):
    return x * _sigmoid(x)


def _softplus(x):
    return jnp.maximum(x, 0.0) + jnp.log(1.0 + jnp.exp(-jnp.abs(x)))


def _layernorm(h, g, b):
    mu = jnp.mean(h, axis=-1, keepdims=True)
    hc = h - mu
    var = jnp.mean(hc * hc, axis=-1, keepdims=True)
    return hc * lax.rsqrt(var + 1e-5) * g + b


def _rms(x, g):
    return x * lax.rsqrt(jnp.mean(x * x, axis=-1, keepdims=True) + 1e-6) * g


def _rope(x, cos, sin_signed, blk):
    lane = lax.broadcasted_iota(jnp.int32, x.shape, 1)
    first = (lane & blk) == 0
    partner = jnp.where(first, pltpu.roll(x, LANES - blk, 1), pltpu.roll(x, blk, 1))
    return x * cos + partner * sin_signed


def _mod_body(s_ref, w_ref, b_ref, o_ref):
    o_ref[0] = _dot(_silu(s_ref[...]), w_ref[0]) + b_ref[0]


def _modulation(cvec, w_mod, b_mod):
    n_layers, d, n = w_mod.shape
    rows = cvec.shape[0]
    tn = 1024
    return pl.pallas_call(
        _mod_body,
        grid=(n_layers, n // tn),
        in_specs=[pl.BlockSpec((rows, d), lambda l, j: (0, 0)),
                  pl.BlockSpec((1, d, tn), lambda l, j: (l, 0, j)),
                  pl.BlockSpec((1, 1, tn), lambda l, j: (l, 0, j))],
        out_specs=pl.BlockSpec((1, rows, tn), lambda l, j: (l, 0, j)),
        out_shape=jax.ShapeDtypeStruct((n_layers, rows, n), f32),
        compiler_params=_cp(2), name="modulation",
    )(cvec, w_mod, b_mod.reshape(n_layers, 1, n))


def _ffn_body(x_ref, mod_ref, wg_ref, wu_ref, wd_ref, g_ref, b_ref, o_ref, hm_ref, *, base, nf):
    f = pl.program_id(1)

    @pl.when(f == 0)
    def _():
        hm_ref[...] = (x_ref[...] * (1.0 + mod_ref[base + 1:base + 2, :]) + mod_ref[base:base + 1, :]).astype(bf16)
        o_ref[...] = jnp.zeros_like(o_ref)

    hm = hm_ref[...]
    g = _dot(hm, wg_ref[...])
    u = _dot(hm, wu_ref[...])
    o_ref[...] += _dot((_silu(g) * u).astype(bf16), wd_ref[...])

    @pl.when(f == nf - 1)
    def _():
        h = ALPHA * x_ref[...] + (0.5 * mod_ref[base + 2:base + 3, :]) * o_ref[...]
        o_ref[...] = _layernorm(h, g_ref[...], b_ref[...])


def _row_tile(rows_per_mod):
    tm = min(512, rows_per_mod)
    assert rows_per_mod % tm == 0
    return tm


def _ffn(x, mods, mod_row0, rows_per_mod, base, wg, wu, wd, l, j, ln_g, ln_b, k):
    m, d = x.shape
    tm = 1024 if rows_per_mod % 1024 == 0 else _row_tile(rows_per_mod)
    tf = 256
    nf = D_FF // tf
    tpm = rows_per_mod // tm
    return pl.pallas_call(
        functools.partial(_ffn_body, base=base, nf=nf),
        grid=(m // tm, nf),
        in_specs=[pl.BlockSpec((tm, d), lambda i, f: (i, 0)),
                  pl.BlockSpec((None, None, N_MOD, d), lambda i, f: (l, mod_row0 + i // tpm, 0, 0)),
                  pl.BlockSpec((None, None, d, tf), lambda i, f: (l, j, 0, f)),
                  pl.BlockSpec((None, None, d, tf), lambda i, f: (l, j, 0, f)),
                  pl.BlockSpec((None, None, tf, d), lambda i, f: (l, j, f, 0)),
                  pl.BlockSpec((None, None, 1, d), lambda i, f: (l, k, 0, 0)),
                  pl.BlockSpec((None, None, 1, d), lambda i, f: (l, k, 0, 0))],
        out_specs=pl.BlockSpec((tm, d), lambda i, f: (i, 0)),
        out_shape=jax.ShapeDtypeStruct((m, d), f32),
        scratch_shapes=[pltpu.VMEM((tm, d), bf16)],
        compiler_params=_cp(2), name="half_ffn",
    )(x, mods, wg, wu, wd, ln_g, ln_b)


def _inproj_body(x_ref, mod_ref, w_ref, ws_ref, o_ref, os_ref, hm_ref):
    @pl.when(pl.program_id(1) == 0)
    def _():
        hm_ref[...] = (x_ref[...] * (1.0 + mod_ref[4:5, :]) + mod_ref[3:4, :]).astype(bf16)
        os_ref[...] = _dot(hm_ref[...], ws_ref[...])

    o_ref[...] = _dot(hm_ref[...], w_ref[...]).astype(o_ref.dtype)


def _inproj(x, mods, mod_row0, rows_per_mod, w, w_small, l):
    m, d = x.shape
    tm = 1024 if rows_per_mod % 1024 == 0 else _row_tile(rows_per_mod)
    tn = 1024
    tpm = rows_per_mod // tm
    return pl.pallas_call(
        _inproj_body,
        grid=(m // tm, D_MAIN // tn),
        in_specs=[pl.BlockSpec((tm, d), lambda i, n: (i, 0)),
                  pl.BlockSpec((None, None, N_MOD, d), lambda i, n: (l, mod_row0 + i // tpm, 0, 0)),
                  pl.BlockSpec((None, d, tn), lambda i, n: (l, 0, n)),
                  pl.BlockSpec((None, d, LANES), lambda i, n: (l, 0, 0))],
        out_specs=(pl.BlockSpec((tm, tn), lambda i, n: (i, n)), pl.BlockSpec((tm, LANES), lambda i, n: (i, 0))),
        out_shape=(jax.ShapeDtypeStruct((m, D_MAIN), bf16), jax.ShapeDtypeStruct((m, LANES), f32)),
        scratch_shapes=[pltpu.VMEM((tm, d), bf16)],
        compiler_params=_cp(2), name="in_proj",
    )(x, mods, w, w_small)


def _outproj_body(x_ref, a_ref, b_ref, c_ref, d_ref, w_ref, mod_ref, g_ref, be_ref, o_ref):
    y = _dot(a_ref[...], w_ref[0:DN_W, :])
    y += _dot(b_ref[...], w_ref[DN_W:DN_W + GQA_W, :])
    y += _dot(c_ref[...], w_ref[DN_W + GQA_W:DN_W + GQA_W + SSD_W, :])
    y += _dot(d_ref[...], w_ref[DN_W + GQA_W + SSD_W:D_MIX, :])
    h = ALPHA * x_ref[...] + mod_ref[5:6, :] * y
    o_ref[...] = _layernorm(h, g_ref[...], be_ref[...])


def _outproj(x, outs, w, mods, mod_row0, rows_per_mod, l, ln_g, ln_b):
    m, d = x.shape
    tm = _row_tile(rows_per_mod)
    tpm = rows_per_mod // tm
    mix_specs = [pl.BlockSpec((tm, o.shape[1]), lambda i: (i, 0)) for o in outs]
    return pl.pallas_call(
        _outproj_body,
        grid=(m // tm,),
        in_specs=[pl.BlockSpec((tm, d), lambda i: (i, 0))] + mix_specs + [
            pl.BlockSpec((None, D_MIX, d), lambda i: (l, 0, 0)),
            pl.BlockSpec((None, None, N_MOD, d), lambda i: (l, mod_row0 + i // tpm, 0, 0)),
            pl.BlockSpec((None, None, 1, d), lambda i: (l, 1, 0, 0)),
            pl.BlockSpec((None, None, 1, d), lambda i: (l, 1, 0, 0))],
        out_specs=pl.BlockSpec((tm, d), lambda i: (i, 0)),
        out_shape=jax.ShapeDtypeStruct((m, d), f32),
        compiler_params=_cp(1), name="out_proj",
    )(x, *outs, w, mods, ln_g, ln_b)


CONV_ROWS = 128
CONV_PAD = 8


def _conv_body(x_ref, w_ref, b_ref, o_ref, pad_ref, *, t, n_q, n_l2):
    cb = pl.program_id(1)
    c = x_ref.shape[1]
    pad_ref[0:CONV_PAD, :] = jnp.zeros((CONV_PAD, c), f32)
    pad_ref[CONV_PAD + t:2 * CONV_PAD + t, :] = jnp.zeros((CONV_PAD, c), f32)
    pad_ref[CONV_PAD:CONV_PAD + t, :] = x_ref[...].astype(f32)
    half = SHORT_CONV // 2
    rows = min(CONV_ROWS, t)
    for c0 in range(0, c, LANES):
        w = w_ref[:, c0:c0 + LANES]
        bias = b_ref[:, c0:c0 + LANES]
        for r0 in range(0, t, rows):
            acc = bias + pad_ref[pl.ds(CONV_PAD - half + r0, rows), c0:c0 + LANES] * w[0:1, :]
            for j in range(1, SHORT_CONV):
                acc += pad_ref[pl.ds(CONV_PAD - half + r0 + j, rows), c0:c0 + LANES] * w[j:j + 1, :]
            y = _silu(acc)
            if n_l2:
                r = lax.rsqrt(jnp.sum(y * y, axis=-1, keepdims=True) + 1e-6)
                y = y * jnp.where(cb < n_q, r * DN_DK ** -0.5, jnp.where(cb < n_l2, r, 1.0))
            o_ref[r0:r0 + rows, c0:c0 + LANES] = y.astype(o_ref.dtype)


def _conv(p, t, col0, n_ch, w, b, n_q, n_l2):
    m = p.shape[0]
    cw = 512
    cb0 = col0 // cw
    return pl.pallas_call(
        functools.partial(_conv_body, t=t, n_q=n_q, n_l2=n_l2),
        grid=(m // t, n_ch // cw),
        in_specs=[pl.BlockSpec((t, cw), lambda i, j: (i, cb0 + j)),
                  pl.BlockSpec((SHORT_CONV, cw), lambda i, j: (0, j)),
                  pl.BlockSpec((1, cw), lambda i, j: (0, j))],
        out_specs=pl.BlockSpec((t, cw), lambda i, j: (i, j)),
        out_shape=jax.ShapeDtypeStruct((m, n_ch), bf16),
        scratch_shapes=[pltpu.VMEM((t + 2 * CONV_PAD, cw), f32)],
        compiler_params=_cp(2), name="short_conv",
    )(p, w, b)


def _attend_transposed(k_s, vt_s, qt16, tk):
    n = 3 if tk % (3 * LANES) == 0 else 1
    ck = tk // n
    s_cur = _dot(k_s[0:ck, :], qt16)
    m = l = acc = None
    for c in range(n):
        s_next = _dot(k_s[(c + 1) * ck:(c + 2) * ck, :], qt16) if c + 1 < n else None
        mc = jnp.max(s_cur, axis=0, keepdims=True)
        m_new = mc if c == 0 else jnp.maximum(m, mc)
        p = jnp.exp2(s_cur - m_new)
        lc = jnp.sum(p, axis=0, keepdims=True)
        pv = _dot(vt_s[:, c * ck:(c + 1) * ck], p.astype(bf16))
        if c == 0:
            l, acc = lc, pv
        else:
            alpha = jnp.exp2(m - m_new)
            l = alpha * l + lc
            acc = alpha * acc + pv
        m, s_cur = m_new, s_next
    return acc, l


def _gqa_body(*refs, tc, tl, rope_q):
    if tl:
        (q_ref, kc_ref, vc_ref, kl_ref, vl_ref, cosq_ref, sinq_ref, cosk_ref, sink_ref,
         qn_ref, kn_ref, o_ref, k_s, v_s) = refs
    else:
        q_ref, kc_ref, vc_ref, qn_ref, kn_ref, o_ref, k_s, v_s = refs

    @pl.when(pl.program_id(2) == 0)
    def _():
        k_s[0:tc, :] = _rms(kc_ref[...].astype(f32), kn_ref[...]).astype(bf16)
        v_s[:, 0:tc] = vc_ref[...].astype(f32).T.astype(bf16)
        if tl:
            rb = min(512, tl)
            for r0 in range(0, tl, rb):
                kk = _rms(kl_ref[r0:r0 + rb, :].astype(f32), kn_ref[...])
                kk = _rope(kk, cosk_ref[r0:r0 + rb, :], sink_ref[r0:r0 + rb, :], GQA_DIM // 4)
                k_s[tc + r0:tc + r0 + rb, :] = kk.astype(bf16)
                v_s[:, tc + r0:tc + r0 + rb] = vl_ref[r0:r0 + rb, :].astype(f32).T.astype(bf16)

    tq = q_ref.shape[0]
    rep = GQA_HEADS // GQA_KV_HEADS
    qts = []
    for r in range(rep):
        q = _rms(q_ref[:, r * GQA_DIM:(r + 1) * GQA_DIM].astype(f32), qn_ref[...])
        if rope_q:
            q = _rope(q, cosq_ref[...], sinq_ref[...], GQA_DIM // 4)
        qts.append((q * (GQA_DIM ** -0.5 * LOG2E)).T)
    acc, l = _attend_transposed(k_s, v_s, jnp.concatenate(qts, axis=1).astype(bf16), tc + tl)
    o_t = acc * (1.0 / l)
    for r in range(rep):
        o_ref[:, r * GQA_DIM:(r + 1) * GQA_DIM] = o_t[:, r * tq:(r + 1) * tq].T.astype(o_ref.dtype)


def _gqa(pq, tq_seq, pc, tc, pl_, tl, rope, qn, kn):
    m = pq.shape[0]
    nb = m // tq_seq
    tq = min(512, tq_seq)
    rep = GQA_HEADS // GQA_KV_HEADS
    qw = rep * GQA_DIM
    qb0 = COL_GQA // qw
    kb0 = (COL_GQA + GQA_HEADS * GQA_DIM) // GQA_DIM
    vb0 = kb0 + GQA_KV_HEADS
    kv_spec = lambda t, b0: pl.BlockSpec((t, GQA_DIM), lambda b, g, i: (b, b0 + g))
    vec = pl.BlockSpec((1, GQA_DIM), lambda b, g, i: (0, 0))
    nq = tq_seq // tq
    in_specs = [pl.BlockSpec((tq, qw), lambda b, g, i: (b * nq + i, qb0 + g)), kv_spec(tc, kb0), kv_spec(tc, vb0)]
    args = [pq, pc, pc]
    if tl:
        cos, sin = rope
        in_specs += [kv_spec(tl, kb0), kv_spec(tl, vb0),
                     pl.BlockSpec((tq, GQA_DIM), lambda b, g, i: (i, 0)), pl.BlockSpec((tq, GQA_DIM), lambda b, g, i: (i, 0)),
                     pl.BlockSpec((tl, GQA_DIM), lambda b, g, i: (0, 0)), pl.BlockSpec((tl, GQA_DIM), lambda b, g, i: (0, 0))]
        args += [pl_, pl_, cos, sin, cos, sin]
    in_specs += [vec, vec]
    args += [qn, kn]
    return pl.pallas_call(
        functools.partial(_gqa_body, tc=tc, tl=tl, rope_q=bool(tl)),
        grid=(nb, GQA_KV_HEADS, nq),
        in_specs=in_specs,
        out_specs=pl.BlockSpec((tq, qw), lambda b, g, i: (b * nq + i, g)),
        out_shape=jax.ShapeDtypeStruct((m, GQA_W), bf16),
        scratch_shapes=[pltpu.VMEM((tc + tl, GQA_DIM), bf16), pltpu.VMEM((GQA_DIM, tc + tl), bf16)],
        compiler_params=_cp(3), name="gqa_attention",
    )(*args)


def _diff_body(*refs, tc, tl, lam_init):
    if tl:
        (q_ref, kc_ref, vc_ref, kl_ref, vl_ref, cosq_ref, sinq_ref, cosk_ref, sink_ref,
         lam_ref, ng_ref, o_ref, k_s, v_s) = refs
    else:
        q_ref, kc_ref, vc_ref, lam_ref, ng_ref, o_ref, k_s, v_s = refs

    @pl.when(pl.program_id(2) == 0)
    def _():
        k_s[0:tc, :] = kc_ref[...]
        v_s[:, 0:tc] = vc_ref[...].astype(f32).T.astype(bf16)
        if tl:
            rb = min(512, tl)
            for r0 in range(0, tl, rb):
                kk = _rope(kl_ref[r0:r0 + rb, :].astype(f32), cosk_ref[r0:r0 + rb, :], sink_ref[r0:r0 + rb, :],
                           DIFF_DIM // 4)
                k_s[tc + r0:tc + r0 + rb, :] = kk.astype(bf16)
                v_s[:, tc + r0:tc + r0 + rb] = vl_ref[r0:r0 + rb, :].astype(f32).T.astype(bf16)

    lv = lam_ref[...]
    lam = (jnp.exp(jnp.sum(lv[0:1, :] * lv[1:2, :], axis=-1, keepdims=True))
           - jnp.exp(jnp.sum(lv[2:3, :] * lv[3:4, :], axis=-1, keepdims=True)) + lam_init)
    q = q_ref[...].astype(f32)
    if tl:
        q = _rope(q, cosq_ref[...], sinq_ref[...], DIFF_DIM // 4)
    tq = q.shape[0]
    qt = (q * (DIFF_DIM ** -0.5 * LOG2E)).T
    row = lax.broadcasted_iota(jnp.int32, qt.shape, 0)
    qt16 = jnp.concatenate([jnp.where(row < DIFF_DIM, qt, 0.0), jnp.where(row >= DIFF_DIM, qt, 0.0)],
                           axis=1).astype(bf16)
    acc, l = _attend_transposed(k_s, v_s, qt16, tc + tl)
    o_t = acc * (1.0 / l)
    o = (o_t[:, 0:tq] - lam * o_t[:, tq:2 * tq]).T
    o_ref[...] = (_rms(o, ng_ref[...]) * (1.0 - lam_init)).astype(o_ref.dtype)


def _diff(pq, tq_seq, pc, tc, pl_, tl, rope, lam_vecs, ng, lam_init):
    m = pq.shape[0]
    nb = m // tq_seq
    tq = min(512, tq_seq)
    nq = tq_seq // tq
    w = DIFF_VDIM
    qb0 = COL_DIFF // w
    kb0 = qb0 + DIFF_HEADS
    vb0 = kb0 + DIFF_HEADS
    kv_spec = lambda t, b0: pl.BlockSpec((t, w), lambda b, h, i: (b, b0 + h))
    in_specs = [pl.BlockSpec((tq, w), lambda b, h, i: (b * nq + i, qb0 + h)), kv_spec(tc, kb0), kv_spec(tc, vb0)]
    args = [pq, pc, pc]
    if tl:
        cos, sin = rope
        in_specs += [kv_spec(tl, kb0), kv_spec(tl, vb0),
                     pl.BlockSpec((tq, w), lambda b, h, i: (i, 0)), pl.BlockSpec((tq, w), lambda b, h, i: (i, 0)),
                     pl.BlockSpec((tl, w), lambda b, h, i: (0, 0)), pl.BlockSpec((tl, w), lambda b, h, i: (0, 0))]
        args += [pl_, pl_, cos, sin, cos, sin]
    in_specs += [pl.BlockSpec((4, DIFF_DIM), lambda b, h, i: (0, 0)), pl.BlockSpec((1, w), lambda b, h, i: (0, 0))]
    args += [lam_vecs, ng]
    return pl.pallas_call(
        functools.partial(_diff_body, tc=tc, tl=tl, lam_init=lam_init),
        grid=(nb, DIFF_HEADS, nq),
        in_specs=in_specs,
        out_specs=pl.BlockSpec((tq, w), lambda b, h, i: (b * nq + i, h)),
        out_shape=jax.ShapeDtypeStruct((m, DIFF_W), bf16),
        scratch_shapes=[pltpu.VMEM((tc + tl, w), bf16), pltpu.VMEM((w, tc + tl), bf16)],
        compiler_params=_cp(3), name="diff_attention",
    )(*args)


def _tri_masks(n):
    row = lax.broadcasted_iota(jnp.int32, (n, n), 0)
    col = lax.broadcasted_iota(jnp.int32, (n, n), 1)
    return row >= col, row <= col, row > col, row < col


def _dn_body(*refs, tc, tl, need_ctx):
    (qc_ref, kc_ref, vc_ref, zc_ref, gc_ref, gtc_ref,
     ql_ref, kl_ref, vl_ref, zl_ref, gl_ref, gtl_ref, cp_ref, rp_ref, ng_ref) = refs[:15]
    if need_ctx:
        oc_ref, ol_ref, o_s, s_s = refs[15:]
    else:
        ol_ref, o_s, s_s = refs[15:]
        oc_ref = None
    c = DN_CHUNK
    ge, le, gt, lt = _tri_masks(c)
    incl = (ge, le)
    strict = (gt, lt)
    ge16 = jnp.where(ge, 1.0, 0.0).astype(bf16)
    le16 = jnp.where(le, 1.0, 0.0).astype(bf16)
    tri_col = (ge16, le16)
    tri_row = (le16, ge16)
    eye = (lax.broadcasted_iota(jnp.int32, (c, c), 0) == lax.broadcasted_iota(jnp.int32, (c, c), 1)).astype(f32)
    neg_a_row = -jnp.exp(cp_ref[0:1, :])
    dtb_row = cp_ref[1:2, :]
    neg_a_col = -jnp.exp(rp_ref[:, 0:1])
    dtb_col = rp_ref[:, 1:2]
    s_s[...] = jnp.zeros_like(s_s)

    def run(q_ref, k_ref, v_ref, g_ref, gt_ref, n_chunks, base):
        def step(i, carry):
            dirs = []
            for d in range(2):
                ci = i if d == 0 else n_chunks - 1 - i
                r = pl.multiple_of(ci * c, c)
                graw = g_ref[pl.ds(r, c), :]
                dirs.append(dict(r=r, beta_all=_sigmoid(graw), gdec=neg_a_row * _softplus(graw + dtb_row),
                                 gdec_r=neg_a_col * _softplus(gt_ref[ci] + dtb_col)))
            for d, dd in enumerate(dirs):
                dd["cum"] = _dot_sel_l(tri_col[d], dd["gdec"], 3)
                dd["cum_r"] = _dot_sel_r(dd["gdec_r"], tri_row[d], 3)
                dd["total"] = jnp.sum(dd["gdec"], axis=0, keepdims=True)
            chains = []
            for d, dd in enumerate(dirs):
                for h in range(DN_HEADS):
                    lb = LANE_DN_B + d * DN_HEADS + h
                    la = LANE_DN_A + d * DN_HEADS + h
                    hs = slice(h * DN_DK, (h + 1) * DN_DK)
                    gcol = dd["cum"][:, la:la + 1]
                    ch = dict(d=d, hs=hs, r=dd["r"], si=d * DN_HEADS + h, beta=dd["beta_all"][:, lb:lb + 1], gcol=gcol,
                              tot=dd["total"][:, la:la + 1], eg=jnp.exp(gcol),
                              decay=jnp.where(incl[d], jnp.exp(jnp.where(incl[d], gcol - dd["cum_r"][la:la + 1, :], 0.0)), 0.0),
                              q16=q_ref[pl.ds(dd["r"], c), hs], k16=k_ref[pl.ds(dd["r"], c), hs],
                              vf=v_ref[pl.ds(dd["r"], c), hs].astype(f32))
                    ch["kf"] = ch["k16"].astype(f32)
                    chains.append(ch)
            for ch in chains:
                kq = _dot_nt(jnp.concatenate([ch["k16"], ch["q16"]], axis=0), ch["k16"])
                ch["kk"] = kq[0:c, :]
                ch["qk"] = (kq[c:2 * c, :] * ch["decay"]).astype(bf16)
            for ch in chains:
                ch["m"] = -jnp.where(strict[ch["d"]], ch["beta"] * ch["kk"] * ch["decay"], 0.0)
                ch["inv"] = eye + ch["m"]
            for _ in range(int(math.log2(c)) - 1):
                for ch in chains:
                    m16 = ch["m"].astype(bf16)
                    ch["m"] = _dot(m16, m16)
                for ch in chains:
                    ch["inv"] = _dot(ch["inv"].astype(bf16), (eye + ch["m"]).astype(bf16))
            for ch in chains:
                rhs = jnp.concatenate([(ch["vf"] * ch["beta"]).astype(bf16),
                                       (ch["kf"] * (ch["beta"] * ch["eg"])).astype(bf16)], axis=1)
                ch["uw"] = _dot(ch["inv"].astype(bf16), rhs)
            for ch in chains:
                ch["s"] = s_s[ch["si"]]
                lhs = jnp.concatenate([ch["uw"][:, DN_DV:DN_DV + DN_DK].astype(bf16),
                                       (ch["q16"].astype(f32) * ch["eg"]).astype(bf16)], axis=0)
                ch["ws"] = _dot(lhs, ch["s"].astype(bf16))
                ch["v16"] = (ch["uw"][:, 0:DN_DV] - ch["ws"][0:c, :]).astype(bf16)
            for ch in chains:
                ch["o"] = ch["ws"][c:2 * c, :] + _dot(ch["qk"], ch["v16"])
            for ch in chains:
                kd = (ch["kf"] * jnp.exp(ch["tot"] - ch["gcol"])).astype(bf16)
                s_s[ch["si"]] = ch["s"] * jnp.exp(ch["tot"]) + _dot_tn(kd, ch["v16"])
            for ch in chains:
                o_s[ch["d"], pl.ds(base + ch["r"], c), ch["hs"]] = ch["o"]
            return carry

        lax.fori_loop(0, n_chunks, step, 0)

    run(qc_ref, kc_ref, vc_ref, gc_ref, gtc_ref, tc // c, 0)
    run(ql_ref, kl_ref, vl_ref, gl_ref, gtl_ref, tl // c, tc)

    def finish(z_ref, out_ref, base, t):
        rb = min(256, t)
        for r0 in range(0, t, rb):
            for h in range(DN_HEADS):
                hs = slice(h * DN_DV, (h + 1) * DN_DV)
                o = o_s[0, base + r0:base + r0 + rb, hs] + o_s[1, base + r0:base + r0 + rb, hs]
                out_ref[r0:r0 + rb, hs] = (_rms(o, ng_ref[...]) * _silu(z_ref[r0:r0 + rb, hs].astype(f32))).astype(out_ref.dtype)

    finish(zl_ref, ol_ref, tc, tl)
    if need_ctx:
        finish(zc_ref, oc_ref, 0, tc)


def _deltanet(qkv_c, p_c, ps_c, gt_c, tc, qkv_l, p_l, ps_l, gt_l, tl, cparams, rparams, ng, need_ctx):
    nb = qkv_l.shape[0] // tl
    w = DN_W
    c = DN_CHUNK

    def seq_specs(t):
        return [pl.BlockSpec((t, w), lambda b: (b, 0)), pl.BlockSpec((t, w), lambda b: (b, 1)),
                pl.BlockSpec((t, w), lambda b: (b, 2)),
                pl.BlockSpec((t, w), lambda b: (b, COL_DN_Z // w)),
                pl.BlockSpec((t, LANES), lambda b: (b, 0)),
                pl.BlockSpec((None, t // c, 32, c), lambda b: (b, 0, 0, 0))]

    in_specs = seq_specs(tc) + seq_specs(tl) + [
        pl.BlockSpec((8, LANES), lambda b: (0, 0)), pl.BlockSpec((32, 2), lambda b: (0, 0)),
        pl.BlockSpec((1, DN_DV), lambda b: (0, 0))]
    out_l = jax.ShapeDtypeStruct((nb * tl, w), bf16)
    out_c = jax.ShapeDtypeStruct((nb * tc, w), bf16)
    spec_l = pl.BlockSpec((tl, w), lambda b: (b, 0))
    spec_c = pl.BlockSpec((tc, w), lambda b: (b, 0))
    res = pl.pallas_call(
        functools.partial(_dn_body, tc=tc, tl=tl, need_ctx=need_ctx),
        grid=(nb,),
        in_specs=in_specs,
        out_specs=(spec_c, spec_l) if need_ctx else spec_l,
        out_shape=(out_c, out_l) if need_ctx else out_l,
        scratch_shapes=[pltpu.VMEM((2, tc + tl, w), f32), pltpu.VMEM((2 * DN_HEADS, DN_DK, DN_DV), f32)],
        compiler_params=_cp(1), name="gated_deltanet",
    )(qkv_c, qkv_c, qkv_c, p_c, ps_c, gt_c, qkv_l, qkv_l, qkv_l, p_l, ps_l, gt_l, cparams, rparams, ng)
    return (res[1], res[0]) if need_ctx else (res, None)


def _ssd_body(*refs, tc, tl, need_ctx):
    (xc_ref, zc_ref, gc_ref, gtc_ref, xl_ref, zl_ref, gl_ref, gtl_ref,
     cp_ref, rp_ref, ex_ref, dsk_ref, ng_ref) = refs[:13]
    if need_ctx:
        oc_ref, ol_ref, y_s, s_s = refs[13:]
    else:
        ol_ref, y_s, s_s = refs[13:]
        oc_ref = None
    c = SSD_CHUNK
    ge, le, _, _ = _tri_masks(c)
    incl = (ge, le)
    ge16 = jnp.where(ge, 1.0, 0.0).astype(bf16)
    le16 = jnp.where(le, 1.0, 0.0).astype(bf16)
    tri_col = (ge16, le16)
    tri_row = (le16, ge16)
    neg_a_row = -jnp.exp(cp_ref[0:1, :])
    dtb_row = cp_ref[1:2, :]
    neg_a_col = -jnp.exp(rp_ref[:, 0:1])
    dtb_col = rp_ref[:, 1:2]
    lane = lax.broadcasted_iota(jnp.int32, (c, LANES), 1)
    lo = lane < SSD_HEADDIM
    hi = lane >= SSD_HEADDIM
    s_s[...] = jnp.zeros_like(s_s)
    hpg = SSD_HEADS // SSD_GROUPS
    gw = hpg * SSD_HEADDIM
    b0 = SSD_W
    c0 = SSD_W + SSD_GROUPS * SSD_STATE

    def run(x_ref, g_ref, gt_ref, n_chunks, base):
        def step(i, carry):
            dirs = []
            for d in range(2):
                ci = i if d == 0 else n_chunks - 1 - i
                r = pl.multiple_of(ci * c, c)
                dt = _softplus(g_ref[pl.ds(r, c), :] + dtb_row)
                dirs.append(dict(r=r, dt=dt, da=dt * neg_a_row, da_r=_softplus(gt_ref[ci] + dtb_col) * neg_a_col))
            for d, dd in enumerate(dirs):
                dd["cum"] = _dot_sel_l(tri_col[d], dd["da"], 3)
                dd["cum_r"] = _dot_sel_r(dd["da_r"], tri_row[d], 3)
                dd["total"] = jnp.sum(dd["da"], axis=0, keepdims=True)
            for d, dd in enumerate(dirs):
                stacked = jnp.concatenate([dd["dt"], dd["dt"] * jnp.exp(dd["total"] - dd["cum"]), jnp.exp(dd["cum"]),
                                           jnp.broadcast_to(jnp.exp(dd["total"]), (8, LANES))], axis=0)
                e_all = _dot_sel_r(stacked, ex_ref[d], 2)
                xs = x_ref[pl.ds(dd["r"], c), 0:SSD_W].astype(f32)
                dd["xdt"] = xs * e_all[0:c, :]
                dd["xdtw"] = (xs * e_all[c:2 * c, :]).astype(bf16)
                dd["cin_e"] = e_all[2 * c:3 * c, :]
                dd["cd_e"] = e_all[3 * c:3 * c + 1, :]
            groups = []
            for d, dd in enumerate(dirs):
                for g in range(SSD_GROUPS):
                    groups.append(dict(d=d, g=g, dd=dd, si=d * SSD_GROUPS + g, gs=slice(g * gw, (g + 1) * gw),
                                       bm=x_ref[pl.ds(dd["r"], c), b0 + g * SSD_STATE:b0 + (g + 1) * SSD_STATE],
                                       cm=x_ref[pl.ds(dd["r"], c), c0 + g * SSD_STATE:c0 + (g + 1) * SSD_STATE]))
            for gg in groups:
                gg["scores"] = _dot_nt(gg["cm"], gg["bm"])
            for gg in groups:
                gg["s"] = s_s[gg["si"]]
                gg["y_off"] = _dot(gg["cm"], gg["s"].astype(bf16)) * gg["dd"]["cin_e"][:, gg["gs"]]
            for gg in groups:
                s_s[gg["si"]] = gg["s"] * gg["dd"]["cd_e"][:, gg["gs"]] + _dot_tn(gg["bm"], gg["dd"]["xdtw"][:, gg["gs"]])
            items = []
            for gg in groups:
                d, g, dd = gg["d"], gg["g"], gg["dd"]
                for pair in range(hpg // 2):
                    ps = slice(g * gw + pair * LANES, g * gw + (pair + 1) * LANES)
                    xp = dd["xdt"][:, ps]
                    for e in range(2):
                        la = LANE_SSD_DT + d * SSD_HEADS + g * hpg + pair * 2 + e
                        seg = dd["cum"][:, la:la + 1] - dd["cum_r"][la:la + 1, :]
                        decay = jnp.where(incl[d], jnp.exp(jnp.where(incl[d], seg, 0.0)), 0.0)
                        items.append(dict(m=(gg["scores"] * decay).astype(bf16),
                                          xh=jnp.where(lo if e == 0 else hi, xp, 0.0).astype(bf16)))
                    gg.setdefault("pairs", []).append((ps, pair, len(items) - 2))
            for it in items:
                it["y"] = _dot(it["m"], it["xh"])
            for gg in groups:
                for ps, pair, k in gg["pairs"]:
                    yp = gg["y_off"][:, pair * LANES:(pair + 1) * LANES] + items[k]["y"] + items[k + 1]["y"]
                    y_s[gg["d"], pl.ds(base + gg["dd"]["r"], c), ps] = yp
            return carry

        lax.fori_loop(0, n_chunks, step, 0)

    run(xc_ref, gc_ref, gtc_ref, tc // c, 0)
    run(xl_ref, gl_ref, gtl_ref, tl // c, tc)

    def finish(x_ref, z_ref, out_ref, base, t):
        rb = min(256, t)
        gcols = SSD_W // SSD_GROUPS
        for r0 in range(0, t, rb):
            for g in range(SSD_GROUPS):
                gs = slice(g * gcols, (g + 1) * gcols)
                y = y_s[0, base + r0:base + r0 + rb, gs] + y_s[1, base + r0:base + r0 + rb, gs]
                y = (y + dsk_ref[:, gs] * x_ref[r0:r0 + rb, gs].astype(f32)) * _silu(z_ref[r0:r0 + rb, gs].astype(f32))
                out_ref[r0:r0 + rb, gs] = _rms(y, ng_ref[:, gs]).astype(out_ref.dtype)

    finish(xl_ref, zl_ref, ol_ref, tc, tl)
    if need_ctx:
        finish(xc_ref, zc_ref, oc_ref, 0, tc)


def _ssd(xbc_c, p_c, ps_c, gt_c, tc, xbc_l, p_l, ps_l, gt_l, tl, cparams, rparams, expand, d_skip, ng, need_ctx):
    nb = xbc_l.shape[0] // tl
    w = SSD_W
    c = SSD_CHUNK

    def seq_specs(t):
        return [pl.BlockSpec((t, SSD_CONV_CH), lambda b: (b, 0)),
                pl.BlockSpec((t, w), lambda b: (b, COL_SSD_Z // w)),
                pl.BlockSpec((t, LANES), lambda b: (b, 0)),
                pl.BlockSpec((None, t // c, 32, c), lambda b: (b, 0, 0, 0))]

    in_specs = seq_specs(tc) + seq_specs(tl) + [
        pl.BlockSpec((8, LANES), lambda b: (0, 0)), pl.BlockSpec((32, 2), lambda b: (0, 0)),
        pl.BlockSpec((2, LANES, w), lambda b: (0, 0, 0)),
        pl.BlockSpec((1, w), lambda b: (0, 0)), pl.BlockSpec((1, w), lambda b: (0, 0))]
    out_l = jax.ShapeDtypeStruct((nb * tl, w), bf16)
    out_c = jax.ShapeDtypeStruct((nb * tc, w), bf16)
    spec_l = pl.BlockSpec((tl, w), lambda b: (b, 0))
    spec_c = pl.BlockSpec((tc, w), lambda b: (b, 0))
    res = pl.pallas_call(
        functools.partial(_ssd_body, tc=tc, tl=tl, need_ctx=need_ctx),
        grid=(nb,),
        in_specs=in_specs,
        out_specs=(spec_c, spec_l) if need_ctx else spec_l,
        out_shape=(out_c, out_l) if need_ctx else out_l,
        scratch_shapes=[pltpu.VMEM((2, tc + tl, w), f32),
                        pltpu.VMEM((2 * SSD_GROUPS, SSD_STATE, w // SSD_GROUPS), f32)],
        compiler_params=_cp(1), name="ssd_scan",
    )(xbc_c, p_c, ps_c, gt_c, xbc_l, p_l, ps_l, gt_l, cparams, rparams, expand, d_skip, ng)
    return (res[1], res[0]) if need_ctx else (res, None)


def _rope_tables(n_tok, head_dim, reps):
    rows = n_tok // GRID_W
    row = jnp.broadcast_to(jnp.arange(rows)[:, None], (rows, GRID_W)).reshape(-1).astype(f32)
    col = jnp.broadcast_to(jnp.arange(GRID_W)[None, :], (rows, GRID_W)).reshape(-1).astype(f32)
    half = head_dim // 2
    inv = ROPE_THETA ** (-jnp.arange(0, half, 2, dtype=f32) / half)
    ar = row[:, None] * inv
    ac = col[:, None] * inv
    cos = jnp.concatenate([jnp.cos(ar), jnp.cos(ar), jnp.cos(ac), jnp.cos(ac)], axis=-1)
    sin = jnp.concatenate([-jnp.sin(ar), jnp.sin(ar), -jnp.sin(ac), jnp.sin(ac)], axis=-1)
    return jnp.tile(cos, (1, reps)), jnp.tile(sin, (1, reps))


W_IN_A_SMALL = DN_CONV_CH + DN_W
W_IN_B0 = W_IN_A_SMALL + 4 * DN_HEADS
W_IN_C_SMALL = W_IN_B0 + (GQA_HEADS + 2 * GQA_KV_HEADS) * GQA_DIM + SSD_W + SSD_CONV_CH
W_IN_D0 = W_IN_C_SMALL + 2 * SSD_HEADS
D_IN = W_IN_D0 + 4 * DIFF_HEADS * DIFF_DIM + DIFF_W


def _pack_body(w_ref, o_ref, s_ref):
    n1 = W_IN_A_SMALL
    n2 = n1 + (W_IN_C_SMALL - W_IN_B0)
    o_ref[:, 0:n1] = w_ref[:, 0:n1].astype(bf16)
    o_ref[:, n1:n2] = w_ref[:, W_IN_B0:W_IN_C_SMALL].astype(bf16)
    o_ref[:, n2:D_MAIN] = w_ref[:, W_IN_D0:D_IN].astype(bf16)
    pad = jnp.zeros((w_ref.shape[0], LANES - 32), f32)
    s_ref[...] = jnp.concatenate([w_ref[:, W_IN_A_SMALL:W_IN_B0], w_ref[:, W_IN_C_SMALL:W_IN_D0], pad], axis=1).astype(bf16)


def _pack_w_in(w_in):
    n_layers, d, d_in = w_in.shape
    assert d_in == D_IN
    tr = 256
    return pl.pallas_call(
        _pack_body,
        grid=(n_layers, d // tr),
        in_specs=[pl.BlockSpec((None, tr, D_IN), lambda l, i: (l, i, 0))],
        out_specs=(pl.BlockSpec((None, tr, D_MAIN), lambda l, i: (l, i, 0)),
                   pl.BlockSpec((None, tr, LANES), lambda l, i: (l, i, 0))),
        out_shape=(jax.ShapeDtypeStruct((n_layers, d, D_MAIN), bf16), jax.ShapeDtypeStruct((n_layers, d, LANES), bf16)),
        compiler_params=_cp(2), name="pack_w_in",
    )(w_in)


def _gates_rowlayout(ps, t, chunk):
    g = ps[:, 0:32]
    nb = g.shape[0] // t
    return g.reshape(nb, t // chunk, chunk, 32).transpose(0, 1, 3, 2)


def _lane_params(a_log, dt_bias, lane0):
    n = a_log.size
    out = jnp.zeros((8, LANES), f32)
    out = out.at[0, lane0:lane0 + n].set(a_log.reshape(-1))
    return out.at[1, lane0:lane0 + n].set(dt_bias.reshape(-1))


def _row_params(a_log, dt_bias, row0):
    n = a_log.size
    out = jnp.zeros((32, 2), f32)
    out = out.at[row0:row0 + n, 0].set(a_log.reshape(-1))
    return out.at[row0:row0 + n, 1].set(dt_bias.reshape(-1))


def _ssd_expand():
    lane = jnp.arange(LANES)[:, None]
    colh = (jnp.arange(SSD_W) // SSD_HEADDIM)[None, :]
    return jnp.stack([(lane == LANE_SSD_DT + d * SSD_HEADS + colh).astype(bf16) for d in range(2)])


def kernel(x, c, ctx, c_ctx, w_mod, b_mod, ln_g, ln_b, ffn_wg, ffn_wu, ffn_wd, w_in, w_out, dn_conv, dn_a_log, dn_dt_bias, dn_norm_g, gqa_q_norm, gqa_k_norm, ssd_conv, ssd_conv_b, ssd_a_log, ssd_dt_bias, ssd_d, ssd_norm_g, diff_lambda, diff_norm_g):
    nb, tl, d = x.shape
    tc = ctx.shape[1]
    n_layers = w_mod.shape[0]
    xl = x.reshape(nb * tl, d)
    xc = ctx.reshape(nb * tc, d)

    rows = -(-(nb + 1) // 8) * 8
    cvec = jnp.zeros((rows, d), f32).at[:nb].set(c).at[nb].set(c_ctx)
    mods = _modulation(cvec, w_mod, b_mod).reshape(n_layers, rows, N_MOD, d)

    wg16, wu16, wd16 = ffn_wg.astype(bf16), ffn_wu.astype(bf16), ffn_wd.astype(bf16)
    win16, win_small16 = _pack_w_in(w_in)
    wout16 = w_out.astype(bf16)
    lng = ln_g.reshape(n_layers, 3, 1, d)
    lnb = ln_b.reshape(n_layers, 3, 1, d)
    rope_gqa = _rope_tables(tl, GQA_DIM, 1)
    rope_diff = _rope_tables(tl, DIFF_DIM, 2)
    ssd_ex = _ssd_expand()
    zero_bias_dn = jnp.zeros((1, DN_CONV_CH), f32)

    lat = dict(mod_row0=0, rows_per_mod=tl)
    cx = dict(mod_row0=nb, rows_per_mod=nb * tc)

    for l in range(n_layers):
        need_ctx = l < n_layers - 1
        lam_init = 0.8 - 0.6 * math.exp(-0.3 * l)
        ffn = lambda h, grp, base, j, k: _ffn(h, mods, grp["mod_row0"], grp["rows_per_mod"], base,
                                              wg16, wu16, wd16, l, j, lng, lnb, k)
        xl = ffn(xl, lat, 0, 0, 0)
        xc = ffn(xc, cx, 0, 0, 0)

        p_l, ps_l = _inproj(xl, mods, 0, tl, win16, win_small16, l)
        p_c, ps_c = _inproj(xc, mods, nb, nb * tc, win16, win_small16, l)

        dn_qkv_l = _conv(p_l, tl, COL_DN_QKV, DN_CONV_CH, dn_conv[l], zero_bias_dn, 1, 2)
        dn_qkv_c = _conv(p_c, tc, COL_DN_QKV, DN_CONV_CH, dn_conv[l], zero_bias_dn, 1, 2)
        a_l, a_c = _deltanet(dn_qkv_c, p_c, ps_c, _gates_rowlayout(ps_c, tc, DN_CHUNK), tc,
                             dn_qkv_l, p_l, ps_l, _gates_rowlayout(ps_l, tl, DN_CHUNK), tl,
                             _lane_params(dn_a_log[l], dn_dt_bias[l], LANE_DN_A),
                             _row_params(dn_a_log[l], dn_dt_bias[l], LANE_DN_A),
                             dn_norm_g[l].reshape(1, DN_DV), need_ctx)
        qn = gqa_q_norm[l].reshape(1, GQA_DIM)
        kn = gqa_k_norm[l].reshape(1, GQA_DIM)
        b_l = _gqa(p_l, tl, p_c, tc, p_l, tl, rope_gqa, qn, kn)
        b_c = _gqa(p_c, tc, p_c, tc, None, 0, None, qn, kn) if need_ctx else None
        ssd_xbc_l = _conv(p_l, tl, COL_SSD_XBC, SSD_CONV_CH, ssd_conv[l], ssd_conv_b[l].reshape(1, -1), 0, 0)
        ssd_xbc_c = _conv(p_c, tc, COL_SSD_XBC, SSD_CONV_CH, ssd_conv[l], ssd_conv_b[l].reshape(1, -1), 0, 0)
        c_l, c_c = _ssd(ssd_xbc_c, p_c, ps_c, _gates_rowlayout(ps_c, tc, SSD_CHUNK), tc,
                        ssd_xbc_l, p_l, ps_l, _gates_rowlayout(ps_l, tl, SSD_CHUNK), tl,
                        _lane_params(ssd_a_log[l], ssd_dt_bias[l], LANE_SSD_DT),
                        _row_params(ssd_a_log[l], ssd_dt_bias[l], LANE_SSD_DT),
                        ssd_ex, jnp.repeat(ssd_d[l], SSD_HEADDIM).reshape(1, SSD_W),
                        ssd_norm_g[l].reshape(1, SSD_W), need_ctx)
        ng = diff_norm_g[l].reshape(1, DIFF_VDIM)
        d_l = _diff(p_l, tl, p_c, tc, p_l, tl, rope_diff, diff_lambda[l], ng, lam_init)
        d_c = _diff(p_c, tc, p_c, tc, None, 0, None, diff_lambda[l], ng, lam_init) if need_ctx else None

        xl = _outproj(xl, (a_l, b_l, c_l, d_l), wout16, mods, 0, tl, l, lng, lnb)
        xl = ffn(xl, lat, 6, 1, 2)
        if need_ctx:
            xc = _outproj(xc, (a_c, b_c, c_c, d_c), wout16, mods, nb, nb * tc, l, lng, lnb)
            xc = ffn(xc, cx, 6, 1, 2)
    return xl.reshape(nb, tl, d)
```

```python
import functools
import math

import jax
import jax.numpy as jnp
from jax import lax
from jax.experimental import pallas as pl
from jax.experimental.pallas import tpu as pltpu

f32 = jnp.float32
bf16 = jnp.bfloat16

D_MODEL = 2048
DEPTH = 2
GRID_W = 64
D_FF = 5632
SHORT_CONV = 5
ROPE_THETA = 10000.0
N_MOD = 9
DN_HEADS, DN_DK, DN_DV, DN_CHUNK = 4, 128, 128, 64
GQA_HEADS, GQA_KV_HEADS, GQA_DIM = 4, 2, 128
SSD_HEADS, SSD_HEADDIM, SSD_GROUPS, SSD_STATE, SSD_CHUNK = 8, 64, 2, 128, 128
DIFF_HEADS, DIFF_DIM, DIFF_VDIM = 4, 64, 128
DN_W = DN_HEADS * DN_DV
GQA_W = GQA_HEADS * GQA_DIM
SSD_W = SSD_HEADS * SSD_HEADDIM
DIFF_W = DIFF_HEADS * DIFF_VDIM
D_MIX = DN_W + GQA_W + SSD_W + DIFF_W
DN_CONV_CH = 2 * DN_HEADS * DN_DK + DN_W
SSD_CONV_CH = SSD_W + 2 * SSD_GROUPS * SSD_STATE
ALPHA = (2 * DEPTH) ** 0.25

COL_DN_QKV = 0
COL_DN_Z = 1536
COL_GQA = 2048
COL_SSD_Z = 3072
COL_SSD_XBC = 3584
COL_DIFF = 4608
D_MAIN = 6144
LANE_DN_B, LANE_DN_A, LANE_SSD_DT = 0, 8, 16
LOG2E = 1.4426950408889634

VMEM_LIMIT = 56 * 1024 * 1024
LANES = 128


def _cp(n_grid):
    return pltpu.CompilerParams(dimension_semantics=("arbitrary",) * n_grid, vmem_limit_bytes=VMEM_LIMIT)


def _dot(a, b):
    return jnp.dot(a, b, preferred_element_type=f32)


def _dot_nt(a, b):
    return lax.dot_general(a, b, (((1,), (1,)), ((), ())), preferred_element_type=f32)


def _dot_tn(a, b):
    return lax.dot_general(a, b, (((0,), (0,)), ((), ())), preferred_element_type=f32)


def _split_bf16(x, parts):
    out = []
    for _ in range(parts - 1):
        hi = x.astype(bf16)
        out.append(hi)
        x = x - hi.astype(f32)
    out.append(x.astype(bf16))
    return out


def _dot_sel_l(sel16, x, parts):
    n = x.shape[1]
    y = _dot(sel16, jnp.concatenate(_split_bf16(x, parts), axis=1))
    return sum(y[:, k * n:(k + 1) * n] for k in range(parts))


def _dot_sel_r(x, sel16, parts):
    m = x.shape[0]
    y = _dot(jnp.concatenate(_split_bf16(x, parts), axis=0), sel16)
    return sum(y[k * m:(k + 1) * m, :] for k in range(parts))


def _sigmoid(x):
    return 1.0 / (1.0 + jnp.exp(-x))


def _silu(x):
    return x * _sigmoid(x)


def _softplus(x):
    return jnp.maximum(x, 0.0) + jnp.log(1.0 + jnp.exp(-jnp.abs(x)))


def _layernorm(h, g, b):
    mu = jnp.mean(h, axis=-1, keepdims=True)
    hc = h - mu
    var = jnp.mean(hc * hc, axis=-1, keepdims=True)
    return hc * lax.rsqrt(var + 1e-5) * g + b


def _rms(x, g):
    return x * lax.rsqrt(jnp.mean(x * x, axis=-1, keepdims=True) + 1e-6) * g


def _rope(x, cos, sin_signed, blk):
    lane = lax.broadcasted_iota(jnp.int32, x.shape, 1)
    first = (lane & blk) == 0
    partner = jnp.where(first, pltpu.roll(x, LANES - blk, 1), pltpu.roll(x, blk, 1))
    return x * cos + partner * sin_signed


def _mod_body(s_ref, w_ref, b_ref, o_ref):
    o_ref[0] = _dot(_silu(s_ref[...]), w_ref[0]) + b_ref[0]


def _modulation(cvec, w_mod, b_mod):
    n_layers, d, n = w_mod.shape
    rows = cvec.shape[0]
    tn = 1024
    return pl.pallas_call(
        _mod_body,
        grid=(n_layers, n // tn),
        in_specs=[pl.BlockSpec((rows, d), lambda l, j: (0, 0)),
                  pl.BlockSpec((1, d, tn), lambda l, j: (l, 0, j)),
                  pl.BlockSpec((1, 1, tn), lambda l, j: (l, 0, j))],
        out_specs=pl.BlockSpec((1, rows, tn), lambda l, j: (l, 0, j)),
        out_shape=jax.ShapeDtypeStruct((n_layers, rows, n), f32),
        compiler_params=_cp(2), name="modulation",
    )(cvec, w_mod, b_mod.reshape(n_layers, 1, n))


def _ffn_body(x_ref, mod_ref, wg_ref, wu_ref, wd_ref, g_ref, b_ref, o_ref, hm_ref, acc_ref, *, base, nf):
    f = pl.program_id(1)

    @pl.when(f == 0)
    def _():
        hm_ref[...] = (x_ref[...] * (1.0 + mod_ref[base + 1:base + 2, :]) + mod_ref[base:base + 1, :]).astype(bf16)
        acc_ref[...] = jnp.zeros_like(acc_ref)

    hm = hm_ref[...]
    g = _dot(hm, wg_ref[...])
    u = _dot(hm, wu_ref[...])
    acc_ref[...] += _dot((_silu(g) * u).astype(bf16), wd_ref[...])

    @pl.when(f == nf - 1)
    def _():
        h = ALPHA * x_ref[...] + (0.5 * mod_ref[base + 2:base + 3, :]) * acc_ref[...]
        o_ref[...] = _layernorm(h, g_ref[...], b_ref[...])


def _row_tile(rows_per_mod):
    tm = min(512, rows_per_mod)
    assert rows_per_mod % tm == 0
    return tm


def _ffn(x, mods, mod_row0, rows_per_mod, base, wg, wu, wd, l, j, ln_g, ln_b, k):
    m, d = x.shape
    tm = _row_tile(rows_per_mod)
    tf = 512
    nf = D_FF // tf
    tpm = rows_per_mod // tm
    return pl.pallas_call(
        functools.partial(_ffn_body, base=base, nf=nf),
        grid=(m // tm, nf),
        in_specs=[pl.BlockSpec((tm, d), lambda i, f: (i, 0)),
                  pl.BlockSpec((None, None, N_MOD, d), lambda i, f: (l, mod_row0 + i // tpm, 0, 0)),
                  pl.BlockSpec((None, None, d, tf), lambda i, f: (l, j, 0, f)),
                  pl.BlockSpec((None, None, d, tf), lambda i, f: (l, j, 0, f)),
                  pl.BlockSpec((None, None, tf, d), lambda i, f: (l, j, f, 0)),
                  pl.BlockSpec((None, None, 1, d), lambda i, f: (l, k, 0, 0)),
                  pl.BlockSpec((None, None, 1, d), lambda i, f: (l, k, 0, 0))],
        out_specs=pl.BlockSpec((tm, d), lambda i, f: (i, 0)),
        out_shape=jax.ShapeDtypeStruct((m, d), f32),
        scratch_shapes=[pltpu.VMEM((tm, d), bf16), pltpu.VMEM((tm, d), f32)],
        compiler_params=_cp(2), name="half_ffn",
    )(x, mods, wg, wu, wd, ln_g, ln_b)


def _inproj_body(x_ref, mod_ref, w_ref, ws_ref, o_ref, os_ref, hm_ref):
    @pl.when(pl.program_id(1) == 0)
    def _():
        hm_ref[...] = (x_ref[...] * (1.0 + mod_ref[4:5, :]) + mod_ref[3:4, :]).astype(bf16)
        os_ref[...] = _dot(hm_ref[...], ws_ref[...])

    o_ref[...] = _dot(hm_ref[...], w_ref[...]).astype(o_ref.dtype)


def _inproj(x, mods, mod_row0, rows_per_mod, w, w_small, l):
    m, d = x.shape
    tm = 1024 if rows_per_mod % 1024 == 0 else _row_tile(rows_per_mod)
    tn = 1024
    tpm = rows_per_mod // tm
    return pl.pallas_call(
        _inproj_body,
        grid=(m // tm, D_MAIN // tn),
        in_specs=[pl.BlockSpec((tm, d), lambda i, n: (i, 0)),
                  pl.BlockSpec((None, None, N_MOD, d), lambda i, n: (l, mod_row0 + i // tpm, 0, 0)),
                  pl.BlockSpec((None, d, tn), lambda i, n: (l, 0, n)),
                  pl.BlockSpec((None, d, LANES), lambda i, n: (l, 0, 0))],
        out_specs=(pl.BlockSpec((tm, tn), lambda i, n: (i, n)), pl.BlockSpec((tm, LANES), lambda i, n: (i, 0))),
        out_shape=(jax.ShapeDtypeStruct((m, D_MAIN), bf16), jax.ShapeDtypeStruct((m, LANES), f32)),
        scratch_shapes=[pltpu.VMEM((tm, d), bf16)],
        compiler_params=_cp(2), name="in_proj",
    )(x, mods, w, w_small)


def _outproj_body(x_ref, a_ref, b_ref, c_ref, d_ref, w_ref, mod_ref, g_ref, be_ref, o_ref):
    y = _dot(a_ref[...], w_ref[0:DN_W, :])
    y += _dot(b_ref[...], w_ref[DN_W:DN_W + GQA_W, :])
    y += _dot(c_ref[...], w_ref[DN_W + GQA_W:DN_W + GQA_W + SSD_W, :])
    y += _dot(d_ref[...], w_ref[DN_W + GQA_W + SSD_W:D_MIX, :])
    h = ALPHA * x_ref[...] + mod_ref[5:6, :] * y
    o_ref[...] = _layernorm(h, g_ref[...], be_ref[...])


def _outproj(x, outs, w, mods, mod_row0, rows_per_mod, l, ln_g, ln_b):
    m, d = x.shape
    tm = _row_tile(rows_per_mod)
    tpm = rows_per_mod // tm
    mix_specs = [pl.BlockSpec((tm, o.shape[1]), lambda i: (i, 0)) for o in outs]
    return pl.pallas_call(
        _outproj_body,
        grid=(m // tm,),
        in_specs=[pl.BlockSpec((tm, d), lambda i: (i, 0))] + mix_specs + [
            pl.BlockSpec((None, D_MIX, d), lambda i: (l, 0, 0)),
            pl.BlockSpec((None, None, N_MOD, d), lambda i: (l, mod_row0 + i // tpm, 0, 0)),
            pl.BlockSpec((None, None, 1, d), lambda i: (l, 1, 0, 0)),
            pl.BlockSpec((None, None, 1, d), lambda i: (l, 1, 0, 0))],
        out_specs=pl.BlockSpec((tm, d), lambda i: (i, 0)),
        out_shape=jax.ShapeDtypeStruct((m, d), f32),
        compiler_params=_cp(1), name="out_proj",
    )(x, *outs, w, mods, ln_g, ln_b)


CONV_ROWS = 128
CONV_PAD = 8


def _conv_body(x_ref, w_ref, b_ref, o_ref, pad_ref, *, t, n_q, n_l2):
    cb = pl.program_id(1)
    c = x_ref.shape[1]
    pad_ref[0:CONV_PAD, :] = jnp.zeros((CONV_PAD, c), f32)
    pad_ref[CONV_PAD + t:2 * CONV_PAD + t, :] = jnp.zeros((CONV_PAD, c), f32)
    pad_ref[CONV_PAD:CONV_PAD + t, :] = x_ref[...].astype(f32)
    half = SHORT_CONV // 2
    rows = min(CONV_ROWS, t)
    for c0 in range(0, c, LANES):
        w = w_ref[:, c0:c0 + LANES]
        bias = b_ref[:, c0:c0 + LANES]
        for r0 in range(0, t, rows):
            acc = bias + pad_ref[pl.ds(CONV_PAD - half + r0, rows), c0:c0 + LANES] * w[0:1, :]
            for j in range(1, SHORT_CONV):
                acc += pad_ref[pl.ds(CONV_PAD - half + r0 + j, rows), c0:c0 + LANES] * w[j:j + 1, :]
            y = _silu(acc)
            if n_l2:
                r = lax.rsqrt(jnp.sum(y * y, axis=-1, keepdims=True) + 1e-6)
                y = y * jnp.where(cb < n_q, r * DN_DK ** -0.5, jnp.where(cb < n_l2, r, 1.0))
            o_ref[r0:r0 + rows, c0:c0 + LANES] = y.astype(o_ref.dtype)


def _conv(p, t, col0, n_ch, w, b, n_q, n_l2):
    m = p.shape[0]
    cw = 512
    cb0 = col0 // cw
    return pl.pallas_call(
        functools.partial(_conv_body, t=t, n_q=n_q, n_l2=n_l2),
        grid=(m // t, n_ch // cw),
        in_specs=[pl.BlockSpec((t, cw), lambda i, j: (i, cb0 + j)),
                  pl.BlockSpec((SHORT_CONV, cw), lambda i, j: (0, j)),
                  pl.BlockSpec((1, cw), lambda i, j: (0, j))],
        out_specs=pl.BlockSpec((t, cw), lambda i, j: (i, j)),
        out_shape=jax.ShapeDtypeStruct((m, n_ch), bf16),
        scratch_shapes=[pltpu.VMEM((t + 2 * CONV_PAD, cw), f32)],
        compiler_params=_cp(2), name="short_conv",
    )(p, w, b)


def _attend_transposed(k_s, vt_s, qt16, tk):
    n = 3 if tk % (3 * LANES) == 0 else 1
    ck = tk // n
    s_cur = _dot(k_s[0:ck, :], qt16)
    m = l = acc = None
    for c in range(n):
        s_next = _dot(k_s[(c + 1) * ck:(c + 2) * ck, :], qt16) if c + 1 < n else None
        mc = jnp.max(s_cur, axis=0, keepdims=True)
        m_new = mc if c == 0 else jnp.maximum(m, mc)
        p = jnp.exp2(s_cur - m_new)
        lc = jnp.sum(p, axis=0, keepdims=True)
        pv = _dot(vt_s[:, c * ck:(c + 1) * ck], p.astype(bf16))
        if c == 0:
            l, acc = lc, pv
        else:
            alpha = jnp.exp2(m - m_new)
            l = alpha * l + lc
            acc = alpha * acc + pv
        m, s_cur = m_new, s_next
    return acc, l


def _gqa_body(*refs, tc, tl, rope_q):
    if tl:
        (q_ref, kc_ref, vc_ref, kl_ref, vl_ref, cosq_ref, sinq_ref, cosk_ref, sink_ref,
         qn_ref, kn_ref, o_ref, k_s, v_s) = refs
    else:
        q_ref, kc_ref, vc_ref, qn_ref, kn_ref, o_ref, k_s, v_s = refs

    @pl.when(pl.program_id(2) == 0)
    def _():
        k_s[0:tc, :] = _rms(kc_ref[...].astype(f32), kn_ref[...]).astype(bf16)
        v_s[:, 0:tc] = vc_ref[...].astype(f32).T.astype(bf16)
        if tl:
            rb = min(512, tl)
            for r0 in range(0, tl, rb):
                kk = _rms(kl_ref[r0:r0 + rb, :].astype(f32), kn_ref[...])
                kk = _rope(kk, cosk_ref[r0:r0 + rb, :], sink_ref[r0:r0 + rb, :], GQA_DIM // 4)
                k_s[tc + r0:tc + r0 + rb, :] = kk.astype(bf16)
                v_s[:, tc + r0:tc + r0 + rb] = vl_ref[r0:r0 + rb, :].astype(f32).T.astype(bf16)

    tq = q_ref.shape[0]
    rep = GQA_HEADS // GQA_KV_HEADS
    qts = []
    for r in range(rep):
        q = _rms(q_ref[:, r * GQA_DIM:(r + 1) * GQA_DIM].astype(f32), qn_ref[...])
        if rope_q:
            q = _rope(q, cosq_ref[...], sinq_ref[...], GQA_DIM // 4)
        qts.append((q * (GQA_DIM ** -0.5 * LOG2E)).T)
    acc, l = _attend_transposed(k_s, v_s, jnp.concatenate(qts, axis=1).astype(bf16), tc + tl)
    o_t = acc * (1.0 / l)
    for r in range(rep):
        o_ref[:, r * GQA_DIM:(r + 1) * GQA_DIM] = o_t[:, r * tq:(r + 1) * tq].T.astype(o_ref.dtype)


def _gqa(pq, tq_seq, pc, tc, pl_, tl, rope, qn, kn):
    m = pq.shape[0]
    nb = m // tq_seq
    tq = min(512, tq_seq)
    rep = GQA_HEADS // GQA_KV_HEADS
    qw = rep * GQA_DIM
    qb0 = COL_GQA // qw
    kb0 = (COL_GQA + GQA_HEADS * GQA_DIM) // GQA_DIM
    vb0 = kb0 + GQA_KV_HEADS
    kv_spec = lambda t, b0: pl.BlockSpec((t, GQA_DIM), lambda b, g, i: (b, b0 + g))
    vec = pl.BlockSpec((1, GQA_DIM), lambda b, g, i: (0, 0))
    nq = tq_seq // tq
    in_specs = [pl.BlockSpec((tq, qw), lambda b, g, i: (b * nq + i, qb0 + g)), kv_spec(tc, kb0), kv_spec(tc, vb0)]
    args = [pq, pc, pc]
    if tl:
        cos, sin = rope
        in_specs += [kv_spec(tl, kb0), kv_spec(tl, vb0),
                     pl.BlockSpec((tq, GQA_DIM), lambda b, g, i: (i, 0)), pl.BlockSpec((tq, GQA_DIM), lambda b, g, i: (i, 0)),
                     pl.BlockSpec((tl, GQA_DIM), lambda b, g, i: (0, 0)), pl.BlockSpec((tl, GQA_DIM), lambda b, g, i: (0, 0))]
        args += [pl_, pl_, cos, sin, cos, sin]
    in_specs += [vec, vec]
    args += [qn, kn]
    return pl.pallas_call(
        functools.partial(_gqa_body, tc=tc, tl=tl, rope_q=bool(tl)),
        grid=(nb, GQA_KV_HEADS, nq),
        in_specs=in_specs,
        out_specs=pl.BlockSpec((tq, qw), lambda b, g, i: (b * nq + i, g)),
        out_shape=jax.ShapeDtypeStruct((m, GQA_W), bf16),
        scratch_shapes=[pltpu.VMEM((tc + tl, GQA_DIM), bf16), pltpu.VMEM((GQA_DIM, tc + tl), bf16)],
        compiler_params=_cp(3), name="gqa_attention",
    )(*args)


def _diff_body(*refs, tc, tl, lam_init):
    if tl:
        (q_ref, kc_ref, vc_ref, kl_ref, vl_ref, cosq_ref, sinq_ref, cosk_ref, sink_ref,
         lam_ref, ng_ref, o_ref, k_s, v_s) = refs
    else:
        q_ref, kc_ref, vc_ref, lam_ref, ng_ref, o_ref, k_s, v_s = refs

    @pl.when(pl.program_id(2) == 0)
    def _():
        k_s[0:tc, :] = kc_ref[...]
        v_s[:, 0:tc] = vc_ref[...].astype(f32).T.astype(bf16)
        if tl:
            rb = min(512, tl)
            for r0 in range(0, tl, rb):
                kk = _rope(kl_ref[r0:r0 + rb, :].astype(f32), cosk_ref[r0:r0 + rb, :], sink_ref[r0:r0 + rb, :],
                           DIFF_DIM // 4)
                k_s[tc + r0:tc + r0 + rb, :] = kk.astype(bf16)
                v_s[:, tc + r0:tc + r0 + rb] = vl_ref[r0:r0 + rb, :].astype(f32).T.astype(bf16)

    lv = lam_ref[...]
    lam = (jnp.exp(jnp.sum(lv[0:1, :] * lv[1:2, :], axis=-1, keepdims=True))
           - jnp.exp(jnp.sum(lv[2:3, :] * lv[3:4, :], axis=-1, keepdims=True)) + lam_init)
    q = q_ref[...].astype(f32)
    if tl:
        q = _rope(q, cosq_ref[...], sinq_ref[...], DIFF_DIM // 4)
    tq = q.shape[0]
    qt = (q * (DIFF_DIM ** -0.5 * LOG2E)).T
    row = lax.broadcasted_iota(jnp.int32, qt.shape, 0)
    qt16 = jnp.concatenate([jnp.where(row < DIFF_DIM, qt, 0.0), jnp.where(row >= DIFF_DIM, qt, 0.0)],
                           axis=1).astype(bf16)
    acc, l = _attend_transposed(k_s, v_s, qt16, tc + tl)
    o_t = acc * (1.0 / l)
    o = (o_t[:, 0:tq] - lam * o_t[:, tq:2 * tq]).T
    o_ref[...] = (_rms(o, ng_ref[...]) * (1.0 - lam_init)).astype(o_ref.dtype)


def _diff(pq, tq_seq, pc, tc, pl_, tl, rope, lam_vecs, ng, lam_init):
    m = pq.shape[0]
    nb = m // tq_seq
    tq = min(512, tq_seq)
    nq = tq_seq // tq
    w = DIFF_VDIM
    qb0 = COL_DIFF // w
    kb0 = qb0 + DIFF_HEADS
    vb0 = kb0 + DIFF_HEADS
    kv_spec = lambda t, b0: pl.BlockSpec((t, w), lambda b, h, i: (b, b0 + h))
    in_specs = [pl.BlockSpec((tq, w), lambda b, h, i: (b * nq + i, qb0 + h)), kv_spec(tc, kb0), kv_spec(tc, vb0)]
    args = [pq, pc, pc]
    if tl:
        cos, sin = rope
        in_specs += [kv_spec(tl, kb0), kv_spec(tl, vb0),
                     pl.BlockSpec((tq, w), lambda b, h, i: (i, 0)), pl.BlockSpec((tq, w), lambda b, h, i: (i, 0)),
                     pl.BlockSpec((tl, w), lambda b, h, i: (0, 0)), pl.BlockSpec((tl, w), lambda b, h, i: (0, 0))]
        args += [pl_, pl_, cos, sin, cos, sin]
    in_specs += [pl.BlockSpec((4, DIFF_DIM), lambda b, h, i: (0, 0)), pl.BlockSpec((1, w), lambda b, h, i: (0, 0))]
    args += [lam_vecs, ng]
    return pl.pallas_call(
        functools.partial(_diff_body, tc=tc, tl=tl, lam_init=lam_init),
        grid=(nb, DIFF_HEADS, nq),
        in_specs=in_specs,
        out_specs=pl.BlockSpec((tq, w), lambda b, h, i: (b * nq + i, h)),
        out_shape=jax.ShapeDtypeStruct((m, DIFF_W), bf16),
        scratch_shapes=[pltpu.VMEM((tc + tl, w), bf16), pltpu.VMEM((w, tc + tl), bf16)],
        compiler_params=_cp(3), name="diff_attention",
    )(*args)


def _tri_masks(n):
    row = lax.broadcasted_iota(jnp.int32, (n, n), 0)
    col = lax.broadcasted_iota(jnp.int32, (n, n), 1)
    return row >= col, row <= col, row > col, row < col


def _dn_body(*refs, tc, tl, need_ctx):
    (qc_ref, kc_ref, vc_ref, zc_ref, gc_ref, gtc_ref,
     ql_ref, kl_ref, vl_ref, zl_ref, gl_ref, gtl_ref, cp_ref, rp_ref, ng_ref) = refs[:15]
    if need_ctx:
        oc_ref, ol_ref, o_s, s_s = refs[15:]
    else:
        ol_ref, o_s, s_s = refs[15:]
        oc_ref = None
    c = DN_CHUNK
    ge, le, gt, lt = _tri_masks(c)
    incl = (ge, le)
    strict = (gt, lt)
    ge16 = jnp.where(ge, 1.0, 0.0).astype(bf16)
    le16 = jnp.where(le, 1.0, 0.0).astype(bf16)
    tri_col = (ge16, le16)
    tri_row = (le16, ge16)
    eye = (lax.broadcasted_iota(jnp.int32, (c, c), 0) == lax.broadcasted_iota(jnp.int32, (c, c), 1)).astype(f32)
    neg_a_row = -jnp.exp(cp_ref[0:1, :])
    dtb_row = cp_ref[1:2, :]
    neg_a_col = -jnp.exp(rp_ref[:, 0:1])
    dtb_col = rp_ref[:, 1:2]
    s_s[...] = jnp.zeros_like(s_s)

    def run(q_ref, k_ref, v_ref, g_ref, gt_ref, n_chunks, base):
        def step(i, carry):
            dirs = []
            for d in range(2):
                ci = i if d == 0 else n_chunks - 1 - i
                r = pl.multiple_of(ci * c, c)
                graw = g_ref[pl.ds(r, c), :]
                dirs.append(dict(r=r, beta_all=_sigmoid(graw), gdec=neg_a_row * _softplus(graw + dtb_row),
                                 gdec_r=neg_a_col * _softplus(gt_ref[ci] + dtb_col)))
            for d, dd in enumerate(dirs):
                dd["cum"] = _dot_sel_l(tri_col[d], dd["gdec"], 3)
                dd["cum_r"] = _dot_sel_r(dd["gdec_r"], tri_row[d], 3)
                dd["total"] = jnp.sum(dd["gdec"], axis=0, keepdims=True)
            chains = []
            for d, dd in enumerate(dirs):
                for h in range(DN_HEADS):
                    lb = LANE_DN_B + d * DN_HEADS + h
                    la = LANE_DN_A + d * DN_HEADS + h
                    hs = slice(h * DN_DK, (h + 1) * DN_DK)
                    gcol = dd["cum"][:, la:la + 1]
                    ch = dict(d=d, hs=hs, r=dd["r"], si=d * DN_HEADS + h, beta=dd["beta_all"][:, lb:lb + 1], gcol=gcol,
                              tot=dd["total"][:, la:la + 1], eg=jnp.exp(gcol),
                              decay=jnp.where(incl[d], jnp.exp(jnp.where(incl[d], gcol - dd["cum_r"][la:la + 1, :], 0.0)), 0.0),
                              q16=q_ref[pl.ds(dd["r"], c), hs], k16=k_ref[pl.ds(dd["r"], c), hs],
                              vf=v_ref[pl.ds(dd["r"], c), hs].astype(f32))
                    ch["kf"] = ch["k16"].astype(f32)
                    chains.append(ch)
            for ch in chains:
                kq = _dot_nt(jnp.concatenate([ch["k16"], ch["q16"]], axis=0), ch["k16"])
                ch["kk"] = kq[0:c, :]
                ch["qk"] = (kq[c:2 * c, :] * ch["decay"]).astype(bf16)
            for ch in chains:
                ch["m"] = -jnp.where(strict[ch["d"]], ch["beta"] * ch["kk"] * ch["decay"], 0.0)
                ch["inv"] = eye + ch["m"]
            for _ in range(int(math.log2(c)) - 1):
                for ch in chains:
                    m16 = ch["m"].astype(bf16)
                    ch["m"] = _dot(m16, m16)
                for ch in chains:
                    ch["inv"] = _dot(ch["inv"].astype(bf16), (eye + ch["m"]).astype(bf16))
            for ch in chains:
                rhs = jnp.concatenate([(ch["vf"] * ch["beta"]).astype(bf16),
                                       (ch["kf"] * (ch["beta"] * ch["eg"])).astype(bf16)], axis=1)
                ch["uw"] = _dot(ch["inv"].astype(bf16), rhs)
            for ch in chains:
                ch["s"] = s_s[ch["si"]]
                lhs = jnp.concatenate([ch["uw"][:, DN_DV:DN_DV + DN_DK].astype(bf16),
                                       (ch["q16"].astype(f32) * ch["eg"]).astype(bf16)], axis=0)
                ch["ws"] = _dot(lhs, ch["s"].astype(bf16))
                ch["v16"] = (ch["uw"][:, 0:DN_DV] - ch["ws"][0:c, :]).astype(bf16)
            for ch in chains:
                ch["o"] = ch["ws"][c:2 * c, :] + _dot(ch["qk"], ch["v16"])
            for ch in chains:
                kd = (ch["kf"] * jnp.exp(ch["tot"] - ch["gcol"])).astype(bf16)
                s_s[ch["si"]] = ch["s"] * jnp.exp(ch["tot"]) + _dot_tn(kd, ch["v16"])
            for ch in chains:
                o_s[ch["d"], pl.ds(base + ch["r"], c), ch["hs"]] = ch["o"]
            return carry

        lax.fori_loop(0, n_chunks, step, 0)

    run(qc_ref, kc_ref, vc_ref, gc_ref, gtc_ref, tc // c, 0)
    run(ql_ref, kl_ref, vl_ref, gl_ref, gtl_ref, tl // c, tc)

    def finish(z_ref, out_ref, base, t):
        rb = min(256, t)
        for r0 in range(0, t, rb):
            for h in range(DN_HEADS):
                hs = slice(h * DN_DV, (h + 1) * DN_DV)
                o = o_s[0, base + r0:base + r0 + rb, hs] + o_s[1, base + r0:base + r0 + rb, hs]
                out_ref[r0:r0 + rb, hs] = (_rms(o, ng_ref[...]) * _silu(z_ref[r0:r0 + rb, hs].astype(f32))).astype(out_ref.dtype)

    finish(zl_ref, ol_ref, tc, tl)
    if need_ctx:
        finish(zc_ref, oc_ref, 0, tc)


def _deltanet(qkv_c, p_c, ps_c, gt_c, tc, qkv_l, p_l, ps_l, gt_l, tl, cparams, rparams, ng, need_ctx):
    nb = qkv_l.shape[0] // tl
    w = DN_W
    c = DN_CHUNK

    def seq_specs(t):
        return [pl.BlockSpec((t, w), lambda b: (b, 0)), pl.BlockSpec((t, w), lambda b: (b, 1)),
                pl.BlockSpec((t, w), lambda b: (b, 2)),
                pl.BlockSpec((t, w), lambda b: (b, COL_DN_Z // w)),
                pl.BlockSpec((t, LANES), lambda b: (b, 0)),
                pl.BlockSpec((None, t // c, 32, c), lambda b: (b, 0, 0, 0))]

    in_specs = seq_specs(tc) + seq_specs(tl) + [
        pl.BlockSpec((8, LANES), lambda b: (0, 0)), pl.BlockSpec((32, 2), lambda b: (0, 0)),
        pl.BlockSpec((1, DN_DV), lambda b: (0, 0))]
    out_l = jax.ShapeDtypeStruct((nb * tl, w), bf16)
    out_c = jax.ShapeDtypeStruct((nb * tc, w), bf16)
    spec_l = pl.BlockSpec((tl, w), lambda b: (b, 0))
    spec_c = pl.BlockSpec((tc, w), lambda b: (b, 0))
    res = pl.pallas_call(
        functools.partial(_dn_body, tc=tc, tl=tl, need_ctx=need_ctx),
        grid=(nb,),
        in_specs=in_specs,
        out_specs=(spec_c, spec_l) if need_ctx else spec_l,
        out_shape=(out_c, out_l) if need_ctx else out_l,
        scratch_shapes=[pltpu.VMEM((2, tc + tl, w), f32), pltpu.VMEM((2 * DN_HEADS, DN_DK, DN_DV), f32)],
        compiler_params=_cp(1), name="gated_deltanet",
    )(qkv_c, qkv_c, qkv_c, p_c, ps_c, gt_c, qkv_l, qkv_l, qkv_l, p_l, ps_l, gt_l, cparams, rparams, ng)
    return (res[1], res[0]) if need_ctx else (res, None)


def _ssd_body(*refs, tc, tl, need_ctx):
    (xc_ref, zc_ref, gc_ref, gtc_ref, xl_ref, zl_ref, gl_ref, gtl_ref,
     cp_ref, rp_ref, ex_ref, dsk_ref, ng_ref) = refs[:13]
    if need_ctx:
        oc_ref, ol_ref, y_s, s_s = refs[13:]
    else:
        ol_ref, y_s, s_s = refs[13:]
        oc_ref = None
    c = SSD_CHUNK
    ge, le, _, _ = _tri_masks(c)
    incl = (ge, le)
    ge16 = jnp.where(ge, 1.0, 0.0).astype(bf16)
    le16 = jnp.where(le, 1.0, 0.0).astype(bf16)
    tri_col = (ge16, le16)
    tri_row = (le16, ge16)
    neg_a_row = -jnp.exp(cp_ref[0:1, :])
    dtb_row = cp_ref[1:2, :]
    neg_a_col = -jnp.exp(rp_ref[:, 0:1])
    dtb_col = rp_ref[:, 1:2]
    lane = lax.broadcasted_iota(jnp.int32, (c, LANES), 1)
    lo = lane < SSD_HEADDIM
    hi = lane >= SSD_HEADDIM
    s_s[...] = jnp.zeros_like(s_s)
    hpg = SSD_HEADS // SSD_GROUPS
    gw = hpg * SSD_HEADDIM
    b0 = SSD_W
    c0 = SSD_W + SSD_GROUPS * SSD_STATE

    def run(x_ref, g_ref, gt_ref, n_chunks, base):
        def step(i, carry):
            dirs = []
            for d in range(2):
                ci = i if d == 0 else n_chunks - 1 - i
                r = pl.multiple_of(ci * c, c)
                dt = _softplus(g_ref[pl.ds(r, c), :] + dtb_row)
                dirs.append(dict(r=r, dt=dt, da=dt * neg_a_row, da_r=_softplus(gt_ref[ci] + dtb_col) * neg_a_col))
            for d, dd in enumerate(dirs):
                dd["cum"] = _dot_sel_l(tri_col[d], dd["da"], 3)
                dd["cum_r"] = _dot_sel_r(dd["da_r"], tri_row[d], 3)
                dd["total"] = jnp.sum(dd["da"], axis=0, keepdims=True)
            for d, dd in enumerate(dirs):
                stacked = jnp.concatenate([dd["dt"], dd["dt"] * jnp.exp(dd["total"] - dd["cum"]), jnp.exp(dd["cum"]),
                                           jnp.broadcast_to(jnp.exp(dd["total"]), (8, LANES))], axis=0)
                e_all = _dot_sel_r(stacked, ex_ref[d], 2)
                xs = x_ref[pl.ds(dd["r"], c), 0:SSD_W].astype(f32)
                dd["xdt"] = xs * e_all[0:c, :]
                dd["xdtw"] = (xs * e_all[c:2 * c, :]).astype(bf16)
                dd["cin_e"] = e_all[2 * c:3 * c, :]
                dd["cd_e"] = e_all[3 * c:3 * c + 1, :]
            groups = []
            for d, dd in enumerate(dirs):
                for g in range(SSD_GROUPS):
                    groups.append(dict(d=d, g=g, dd=dd, si=d * SSD_GROUPS + g, gs=slice(g * gw, (g + 1) * gw),
                                       bm=x_ref[pl.ds(dd["r"], c), b0 + g * SSD_STATE:b0 + (g + 1) * SSD_STATE],
                                       cm=x_ref[pl.ds(dd["r"], c), c0 + g * SSD_STATE:c0 + (g + 1) * SSD_STATE]))
            for gg in groups:
                gg["scores"] = _dot_nt(gg["cm"], gg["bm"])
            for gg in groups:
                gg["s"] = s_s[gg["si"]]
                gg["y_off"] = _dot(gg["cm"], gg["s"].astype(bf16)) * gg["dd"]["cin_e"][:, gg["gs"]]
            for gg in groups:
                s_s[gg["si"]] = gg["s"] * gg["dd"]["cd_e"][:, gg["gs"]] + _dot_tn(gg["bm"], gg["dd"]["xdtw"][:, gg["gs"]])
            items = []
            for gg in groups:
                d, g, dd = gg["d"], gg["g"], gg["dd"]
                for pair in range(hpg // 2):
                    ps = slice(g * gw + pair * LANES, g * gw + (pair + 1) * LANES)
                    xp = dd["xdt"][:, ps]
                    for e in range(2):
                        la = LANE_SSD_DT + d * SSD_HEADS + g * hpg + pair * 2 + e
                        seg = dd["cum"][:, la:la + 1] - dd["cum_r"][la:la + 1, :]
                        decay = jnp.where(incl[d], jnp.exp(jnp.where(incl[d], seg, 0.0)), 0.0)
                        items.append(dict(m=(gg["scores"] * decay).astype(bf16),
                                          xh=jnp.where(lo if e == 0 else hi, xp, 0.0).astype(bf16)))
                    gg.setdefault("pairs", []).append((ps, pair, len(items) - 2))
            for it in items:
                it["y"] = _dot(it["m"], it["xh"])
            for gg in groups:
                for ps, pair, k in gg["pairs"]:
                    yp = gg["y_off"][:, pair * LANES:(pair + 1) * LANES] + items[k]["y"] + items[k + 1]["y"]
                    y_s[gg["d"], pl.ds(base + gg["dd"]["r"], c), ps] = yp
            return carry

        lax.fori_loop(0, n_chunks, step, 0)

    run(xc_ref, gc_ref, gtc_ref, tc // c, 0)
    run(xl_ref, gl_ref, gtl_ref, tl // c, tc)

    def finish(x_ref, z_ref, out_ref, base, t):
        rb = min(256, t)
        gcols = SSD_W // SSD_GROUPS
        for r0 in range(0, t, rb):
            for g in range(SSD_GROUPS):
                gs = slice(g * gcols, (g + 1) * gcols)
                y = y_s[0, base + r0:base + r0 + rb, gs] + y_s[1, base + r0:base + r0 + rb, gs]
                y = (y + dsk_ref[:, gs] * x_ref[r0:r0 + rb, gs].astype(f32)) * _silu(z_ref[r0:r0 + rb, gs].astype(f32))
                out_ref[r0:r0 + rb, gs] = _rms(y, ng_ref[:, gs]).astype(out_ref.dtype)

    finish(xl_ref, zl_ref, ol_ref, tc, tl)
    if need_ctx:
        finish(xc_ref, zc_ref, oc_ref, 0, tc)


def _ssd(xbc_c, p_c, ps_c, gt_c, tc, xbc_l, p_l, ps_l, gt_l, tl, cparams, rparams, expand, d_skip, ng, need_ctx):
    nb = xbc_l.shape[0] // tl
    w = SSD_W
    c = SSD_CHUNK

    def seq_specs(t):
        return [pl.BlockSpec((t, SSD_CONV_CH), lambda b: (b, 0)),
                pl.BlockSpec((t, w), lambda b: (b, COL_SSD_Z // w)),
                pl.BlockSpec((t, LANES), lambda b: (b, 0)),
                pl.BlockSpec((None, t // c, 32, c), lambda b: (b, 0, 0, 0))]

    in_specs = seq_specs(tc) + seq_specs(tl) + [
        pl.BlockSpec((8, LANES), lambda b: (0, 0)), pl.BlockSpec((32, 2), lambda b: (0, 0)),
        pl.BlockSpec((2, LANES, w), lambda b: (0, 0, 0)),
        pl.BlockSpec((1, w), lambda b: (0, 0)), pl.BlockSpec((1, w), lambda b: (0, 0))]
    out_l = jax.ShapeDtypeStruct((nb * tl, w), bf16)
    out_c = jax.ShapeDtypeStruct((nb * tc, w), bf16)
    spec_l = pl.BlockSpec((tl, w), lambda b: (b, 0))
    spec_c = pl.BlockSpec((tc, w), lambda b: (b, 0))
    res = pl.pallas_call(
        functools.partial(_ssd_body, tc=tc, tl=tl, need_ctx=need_ctx),
        grid=(nb,),
        in_specs=in_specs,
        out_specs=(spec_c, spec_l) if need_ctx else spec_l,
        out_shape=(out_c, out_l) if need_ctx else out_l,
        scratch_shapes=[pltpu.VMEM((2, tc + tl, w), f32),
                        pltpu.VMEM((2 * SSD_GROUPS, SSD_STATE, w // SSD_GROUPS), f32)],
        compiler_params=_cp(1), name="ssd_scan",
    )(xbc_c, p_c, ps_c, gt_c, xbc_l, p_l, ps_l, gt_l, cparams, rparams, expand, d_skip, ng)
    return (res[1], res[0]) if need_ctx else (res, None)


def _rope_tables(n_tok, head_dim, reps):
    rows = n_tok // GRID_W
    row = jnp.broadcast_to(jnp.arange(rows)[:, None], (rows, GRID_W)).reshape(-1).astype(f32)
    col = jnp.broadcast_to(jnp.arange(GRID_W)[None, :], (rows, GRID_W)).reshape(-1).astype(f32)
    half = head_dim // 2
    inv = ROPE_THETA ** (-jnp.arange(0, half, 2, dtype=f32) / half)
    ar = row[:, None] * inv
    ac = col[:, None] * inv
    cos = jnp.concatenate([jnp.cos(ar), jnp.cos(ar), jnp.cos(ac), jnp.cos(ac)], axis=-1)
    sin = jnp.concatenate([-jnp.sin(ar), jnp.sin(ar), -jnp.sin(ac), jnp.sin(ac)], axis=-1)
    return jnp.tile(cos, (1, reps)), jnp.tile(sin, (1, reps))


W_IN_A_SMALL = DN_CONV_CH + DN_W
W_IN_B0 = W_IN_A_SMALL + 4 * DN_HEADS
W_IN_C_SMALL = W_IN_B0 + (GQA_HEADS + 2 * GQA_KV_HEADS) * GQA_DIM + SSD_W + SSD_CONV_CH
W_IN_D0 = W_IN_C_SMALL + 2 * SSD_HEADS
D_IN = W_IN_D0 + 4 * DIFF_HEADS * DIFF_DIM + DIFF_W


PACK_CHUNK = 512


def _pack_body(wt_ref, o_ref, s_ref):
    tr = wt_ref.shape[1]
    n1 = W_IN_A_SMALL
    n2 = n1 + (W_IN_C_SMALL - W_IN_B0)
    for dst0, src0, n in ((0, 0, n1), (n1, W_IN_B0, n2 - n1), (n2, W_IN_D0, D_MAIN - n2)):
        for c0 in range(0, n, PACK_CHUNK):
            o_ref[:, dst0 + c0:dst0 + c0 + PACK_CHUNK] = wt_ref[src0 + c0:src0 + c0 + PACK_CHUNK, :].T.astype(bf16)
    small = jnp.concatenate([wt_ref[W_IN_A_SMALL:W_IN_B0, :], wt_ref[W_IN_C_SMALL:W_IN_D0, :],
                             jnp.zeros((LANES - 32, tr), f32)], axis=0)
    s_ref[...] = small.T.astype(bf16)


def _pack_w_in(w_in):
    n_layers, d, d_in = w_in.shape
    assert d_in == D_IN
    tr = 256
    return pl.pallas_call(
        _pack_body,
        grid=(n_layers, d // tr),
        in_specs=[pl.BlockSpec((None, D_IN, tr), lambda l, i: (l, 0, i))],
        out_specs=(pl.BlockSpec((None, tr, D_MAIN), lambda l, i: (l, i, 0)),
                   pl.BlockSpec((None, tr, LANES), lambda l, i: (l, i, 0))),
        out_shape=(jax.ShapeDtypeStruct((n_layers, d, D_MAIN), bf16), jax.ShapeDtypeStruct((n_layers, d, LANES), bf16)),
        compiler_params=_cp(2), name="pack_w_in",
    )(jnp.swapaxes(w_in, 1, 2))


def _gates_rowlayout(ps, t, chunk):
    g = ps[:, 0:32]
    nb = g.shape[0] // t
    return g.reshape(nb, t // chunk, chunk, 32).transpose(0, 1, 3, 2)


def _lane_params(a_log, dt_bias, lane0):
    n = a_log.size
    out = jnp.zeros((8, LANES), f32)
    out = out.at[0, lane0:lane0 + n].set(a_log.reshape(-1))
    return out.at[1, lane0:lane0 + n].set(dt_bias.reshape(-1))


def _row_params(a_log, dt_bias, row0):
    n = a_log.size
    out = jnp.zeros((32, 2), f32)
    out = out.at[row0:row0 + n, 0].set(a_log.reshape(-1))
    return out.at[row0:row0 + n, 1].set(dt_bias.reshape(-1))


def _ssd_expand():
    lane = jnp.arange(LANES)[:, None]
    colh = (jnp.arange(SSD_W) // SSD_HEADDIM)[None, :]
    return jnp.stack([(lane == LANE_SSD_DT + d * SSD_HEADS + colh).astype(bf16) for d in range(2)])


def kernel(x, c, ctx, c_ctx, w_mod, b_mod, ln_g, ln_b, ffn_wg, ffn_wu, ffn_wd, w_in, w_out, dn_conv, dn_a_log, dn_dt_bias, dn_norm_g, gqa_q_norm, gqa_k_norm, ssd_conv, ssd_conv_b, ssd_a_log, ssd_dt_bias, ssd_d, ssd_norm_g, diff_lambda, diff_norm_g):
    nb, tl, d = x.shape
    tc = ctx.shape[1]
    n_layers = w_mod.shape[0]
    xl = x.reshape(nb * tl, d)
    xc = ctx.reshape(nb * tc, d)

    rows = -(-(nb + 1) // 8) * 8
    cvec = jnp.zeros((rows, d), f32).at[:nb].set(c).at[nb].set(c_ctx)
    mods = _modulation(cvec, w_mod, b_mod).reshape(n_layers, rows, N_MOD, d)

    wg16, wu16, wd16 = ffn_wg.astype(bf16), ffn_wu.astype(bf16), ffn_wd.astype(bf16)
    win16, win_small16 = _pack_w_in(w_in)
    wout16 = w_out.astype(bf16)
    lng = ln_g.reshape(n_layers, 3, 1, d)
    lnb = ln_b.reshape(n_layers, 3, 1, d)
    rope_gqa = _rope_tables(tl, GQA_DIM, 1)
    rope_diff = _rope_tables(tl, DIFF_DIM, 2)
    ssd_ex = _ssd_expand()
    zero_bias_dn = jnp.zeros((1, DN_CONV_CH), f32)

    lat = dict(mod_row0=0, rows_per_mod=tl)
    cx = dict(mod_row0=nb, rows_per_mod=nb * tc)

    for l in range(n_layers):
        need_ctx = l < n_layers - 1
        lam_init = 0.8 - 0.6 * math.exp(-0.3 * l)
        ffn = lambda h, grp, base, j, k: _ffn(h, mods, grp["mod_row0"], grp["rows_per_mod"], base,
                                              wg16, wu16, wd16, l, j, lng, lnb, k)
        xl = ffn(xl, lat, 0, 0, 0)
        xc = ffn(xc, cx, 0, 0, 0)

        p_l, ps_l = _inproj(xl, mods, 0, tl, win16, win_small16, l)
        p_c, ps_c = _inproj(xc, mods, nb, nb * tc, win16, win_small16, l)

        dn_qkv_l = _conv(p_l, tl, COL_DN_QKV, DN_CONV_CH, dn_conv[l], zero_bias_dn, 1, 2)
        dn_qkv_c = _conv(p_c, tc, COL_DN_QKV, DN_CONV_CH, dn_conv[l], zero_bias_dn, 1, 2)
        a_l, a_c = _deltanet(dn_qkv_c, p_c, ps_c, _gates_rowlayout(ps_c, tc, DN_CHUNK), tc,
                             dn_qkv_l, p_l, ps_l, _gates_rowlayout(ps_l, tl, DN_CHUNK), tl,
                             _lane_params(dn_a_log[l], dn_dt_bias[l], LANE_DN_A),
                             _row_params(dn_a_log[l], dn_dt_bias[l], LANE_DN_A),
                             dn_norm_g[l].reshape(1, DN_DV), need_ctx)
        qn = gqa_q_norm[l].reshape(1, GQA_DIM)
        kn = gqa_k_norm[l].reshape(1, GQA_DIM)
        b_l = _gqa(p_l, tl, p_c, tc, p_l, tl, rope_gqa, qn, kn)
        b_c = _gqa(p_c, tc, p_c, tc, None, 0, None, qn, kn) if need_ctx else None
        ssd_xbc_l = _conv(p_l, tl, COL_SSD_XBC, SSD_CONV_CH, ssd_conv[l], ssd_conv_b[l].reshape(1, -1), 0, 0)
        ssd_xbc_c = _conv(p_c, tc, COL_SSD_XBC, SSD_CONV_CH, ssd_conv[l], ssd_conv_b[l].reshape(1, -1), 0, 0)
        c_l, c_c = _ssd(ssd_xbc_c, p_c, ps_c, _gates_rowlayout(ps_c, tc, SSD_CHUNK), tc,
                        ssd_xbc_l, p_l, ps_l, _gates_rowlayout(ps_l, tl, SSD_CHUNK), tl,
                        _lane_params(ssd_a_log[l], ssd_dt_bias[l], LANE_SSD_DT),
                        _row_params(ssd_a_log[l], ssd_dt_bias[l], LANE_SSD_DT),
                        ssd_ex, jnp.repeat(ssd_d[l], SSD_HEADDIM).reshape(1, SSD_W),
                        ssd_norm_g[l].reshape(1, SSD_W), need_ctx)
        ng = diff_norm_g[l].reshape(1, DIFF_VDIM)
        d_l = _diff(p_l, tl, p_c, tc, p_l, tl, rope_diff, diff_lambda[l], ng, lam_init)
        d_c = _diff(p_c, tc, p_c, tc, None, 0, None, diff_lambda[l], ng, lam_init) if need_ctx else None

        xl = _outproj(xl, (a_l, b_l, c_l, d_l), wout16, mods, 0, tl, l, lng, lnb)
        xl = ffn(xl, lat, 6, 1, 2)
        if need_ctx:
            xc = _outproj(xc, (a_c, b_c, c_c, d_c), wout16, mods, nb, nb * tc, l, lng, lnb)
            xc = ffn(xc, cx, 6, 1, 2)
    return xl.reshape(nb, tl, d)
```

```python
import functools
import math

import jax
import jax.numpy as jnp
from jax import lax
from jax.experimental import pallas as pl
from jax.experimental.pallas import tpu as pltpu

f32 = jnp.float32
bf16 = jnp.bfloat16

D_MODEL = 2048
DEPTH = 2
GRID_W = 64
D_FF = 5632
SHORT_CONV = 5
ROPE_THETA = 10000.0
N_MOD = 9
DN_HEADS, DN_DK, DN_DV, DN_CHUNK = 4, 128, 128, 64
GQA_HEADS, GQA_KV_HEADS, GQA_DIM = 4, 2, 128
SSD_HEADS, SSD_HEADDIM, SSD_GROUPS, SSD_STATE, SSD_CHUNK = 8, 64, 2, 128, 128
DIFF_HEADS, DIFF_DIM, DIFF_VDIM = 4, 64, 128
DN_W = DN_HEADS * DN_DV
GQA_W = GQA_HEADS * GQA_DIM
SSD_W = SSD_HEADS * SSD_HEADDIM
DIFF_W = DIFF_HEADS * DIFF_VDIM
D_MIX = DN_W + GQA_W + SSD_W + DIFF_W
DN_CONV_CH = 2 * DN_HEADS * DN_DK + DN_W
SSD_CONV_CH = SSD_W + 2 * SSD_GROUPS * SSD_STATE
ALPHA = (2 * DEPTH) ** 0.25

COL_DN_QKV = 0
COL_DN_Z = 1536
COL_GQA = 2048
COL_SSD_Z = 3072
COL_SSD_XBC = 3584
COL_DIFF = 4608
D_MAIN = 6144
LANE_DN_B, LANE_DN_A, LANE_SSD_DT = 0, 8, 16
LOG2E = 1.4426950408889634

VMEM_LIMIT = 56 * 1024 * 1024
LANES = 128


def _cp(n_grid):
    return pltpu.CompilerParams(dimension_semantics=("arbitrary",) * n_grid, vmem_limit_bytes=VMEM_LIMIT)


def _dot(a, b):
    return jnp.dot(a, b, preferred_element_type=f32)


def _dot_nt(a, b):
    return lax.dot_general(a, b, (((1,), (1,)), ((), ())), preferred_element_type=f32)


def _dot_tn(a, b):
    return lax.dot_general(a, b, (((0,), (0,)), ((), ())), preferred_element_type=f32)


def _split_bf16(x, parts):
    out = []
    for _ in range(parts - 1):
        hi = x.astype(bf16)
        out.append(hi)
        x = x - hi.astype(f32)
    out.append(x.astype(bf16))
    return out


def _dot_sel_l(sel16, x, parts):
    n = x.shape[1]
    y = _dot(sel16, jnp.concatenate(_split_bf16(x, parts), axis=1))
    return sum(y[:, k * n:(k + 1) * n] for k in range(parts))


def _dot_sel_r(x, sel16, parts):
    m = x.shape[0]
    y = _dot(jnp.concatenate(_split_bf16(x, parts), axis=0), sel16)
    return sum(y[k * m:(k + 1) * m, :] for k in range(parts))


def _sigmoid(x):
    return 1.0 / (1.0 + jnp.exp(-x))


def _silu(x):
    return x * _sigmoid(x)


def _softplus(x):
    return jnp.maximum(x, 0.0) + jnp.log(1.0 + jnp.exp(-jnp.abs(x)))


def _layernorm(h, g, b):
    mu = jnp.mean(h, axis=-1, keepdims=True)
    hc = h - mu
    var = jnp.mean(hc * hc, axis=-1, keepdims=True)
    return hc * lax.rsqrt(var + 1e-5) * g + b


def _rms(x, g):
    return x * lax.rsqrt(jnp.mean(x * x, axis=-1, keepdims=True) + 1e-6) * g


def _rope(x, cos, sin_signed, blk):
    lane = lax.broadcasted_iota(jnp.int32, x.shape, 1)
    first = (lane & blk) == 0
    partner = jnp.where(first, pltpu.roll(x, LANES - blk, 1), pltpu.roll(x, blk, 1))
    return x * cos + partner * sin_signed


def _mod_body(s_ref, w_ref, b_ref, o_ref):
    o_ref[0] = _dot(_silu(s_ref[...]), w_ref[0]) + b_ref[0]


def _modulation(cvec, w_mod, b_mod):
    n_layers, d, n = w_mod.shape
    rows = cvec.shape[0]
    tn = 1024
    return pl.pallas_call(
        _mod_body,
        grid=(n_layers, n // tn),
        in_specs=[pl.BlockSpec((rows, d), lambda l, j: (0, 0)),
                  pl.BlockSpec((1, d, tn), lambda l, j: (l, 0, j)),
                  pl.BlockSpec((1, 1, tn), lambda l, j: (l, 0, j))],
        out_specs=pl.BlockSpec((1, rows, tn), lambda l, j: (l, 0, j)),
        out_shape=jax.ShapeDtypeStruct((n_layers, rows, n), f32),
        compiler_params=_cp(2), name="modulation",
    )(cvec, w_mod, b_mod.reshape(n_layers, 1, n))


def _ffn_body(x_ref, mod_ref, wg_ref, wu_ref, wd_ref, g_ref, b_ref, o_ref, hm_ref, acc_ref, *, base, nf):
    f = pl.program_id(1)

    @pl.when(f == 0)
    def _():
        hm_ref[...] = (x_ref[...] * (1.0 + mod_ref[base + 1:base + 2, :]) + mod_ref[base:base + 1, :]).astype(bf16)
        acc_ref[...] = jnp.zeros_like(acc_ref)

    hm = hm_ref[...]
    g = _dot(hm, wg_ref[...])
    u = _dot(hm, wu_ref[...])
    acc_ref[...] += _dot((_silu(g) * u).astype(bf16), wd_ref[...])

    @pl.when(f == nf - 1)
    def _():
        h = ALPHA * x_ref[...] + (0.5 * mod_ref[base + 2:base + 3, :]) * acc_ref[...]
        o_ref[...] = _layernorm(h, g_ref[...], b_ref[...])


def _row_tile(rows_per_mod):
    tm = min(512, rows_per_mod)
    assert rows_per_mod % tm == 0
    return tm


def _ffn(x, mods, mod_row0, rows_per_mod, base, wg, wu, wd, l, j, ln_g, ln_b, k):
    m, d = x.shape
    tm = _row_tile(rows_per_mod)
    tf = 512
    nf = D_FF // tf
    tpm = rows_per_mod // tm
    return pl.pallas_call(
        functools.partial(_ffn_body, base=base, nf=nf),
        grid=(m // tm, nf),
        in_specs=[pl.BlockSpec((tm, d), lambda i, f: (i, 0)),
                  pl.BlockSpec((None, None, N_MOD, d), lambda i, f: (l, mod_row0 + i // tpm, 0, 0)),
                  pl.BlockSpec((None, None, d, tf), lambda i, f: (l, j, 0, f)),
                  pl.BlockSpec((None, None, d, tf), lambda i, f: (l, j, 0, f)),
                  pl.BlockSpec((None, None, tf, d), lambda i, f: (l, j, f, 0)),
                  pl.BlockSpec((None, None, 1, d), lambda i, f: (l, k, 0, 0)),
                  pl.BlockSpec((None, None, 1, d), lambda i, f: (l, k, 0, 0))],
        out_specs=pl.BlockSpec((tm, d), lambda i, f: (i, 0)),
        out_shape=jax.ShapeDtypeStruct((m, d), f32),
        scratch_shapes=[pltpu.VMEM((tm, d), bf16), pltpu.VMEM((tm, d), f32)],
        compiler_params=_cp(2), name="half_ffn",
    )(x, mods, wg, wu, wd, ln_g, ln_b)


def _inproj_body(x_ref, mod_ref, w_ref, ws_ref, o_ref, os_ref, hm_ref):
    @pl.when(pl.program_id(1) == 0)
    def _():
        hm_ref[...] = (x_ref[...] * (1.0 + mod_ref[4:5, :]) + mod_ref[3:4, :]).astype(bf16)
        os_ref[...] = _dot(hm_ref[...], ws_ref[...])

    o_ref[...] = _dot(hm_ref[...], w_ref[...]).astype(o_ref.dtype)


def _inproj(x, mods, mod_row0, rows_per_mod, w, w_small, l):
    m, d = x.shape
    tm = 1024 if rows_per_mod % 1024 == 0 else _row_tile(rows_per_mod)
    tn = 1024
    tpm = rows_per_mod // tm
    return pl.pallas_call(
        _inproj_body,
        grid=(m // tm, D_MAIN // tn),
        in_specs=[pl.BlockSpec((tm, d), lambda i, n: (i, 0)),
                  pl.BlockSpec((None, None, N_MOD, d), lambda i, n: (l, mod_row0 + i // tpm, 0, 0)),
                  pl.BlockSpec((None, d, tn), lambda i, n: (l, 0, n)),
                  pl.BlockSpec((None, d, LANES), lambda i, n: (l, 0, 0))],
        out_specs=(pl.BlockSpec((tm, tn), lambda i, n: (i, n)), pl.BlockSpec((tm, LANES), lambda i, n: (i, 0))),
        out_shape=(jax.ShapeDtypeStruct((m, D_MAIN), bf16), jax.ShapeDtypeStruct((m, LANES), f32)),
        scratch_shapes=[pltpu.VMEM((tm, d), bf16)],
        compiler_params=_cp(2), name="in_proj",
    )(x, mods, w, w_small)


def _outproj_body(x_ref, a_ref, b_ref, c_ref, d_ref, w_ref, mod_ref, g_ref, be_ref, o_ref):
    y = _dot(a_ref[...], w_ref[0:DN_W, :])
    y += _dot(b_ref[...], w_ref[DN_W:DN_W + GQA_W, :])
    y += _dot(c_ref[...], w_ref[DN_W + GQA_W:DN_W + GQA_W + SSD_W, :])
    y += _dot(d_ref[...], w_ref[DN_W + GQA_W + SSD_W:D_MIX, :])
    h = ALPHA * x_ref[...] + mod_ref[5:6, :] * y
    o_ref[...] = _layernorm(h, g_ref[...], be_ref[...])


def _outproj(x, outs, w, mods, mod_row0, rows_per_mod, l, ln_g, ln_b):
    m, d = x.shape
    tm = _row_tile(rows_per_mod)
    tpm = rows_per_mod // tm
    mix_specs = [pl.BlockSpec((tm, o.shape[1]), lambda i: (i, 0)) for o in outs]
    return pl.pallas_call(
        _outproj_body,
        grid=(m // tm,),
        in_specs=[pl.BlockSpec((tm, d), lambda i: (i, 0))] + mix_specs + [
            pl.BlockSpec((None, D_MIX, d), lambda i: (l, 0, 0)),
            pl.BlockSpec((None, None, N_MOD, d), lambda i: (l, mod_row0 + i // tpm, 0, 0)),
            pl.BlockSpec((None, None, 1, d), lambda i: (l, 1, 0, 0)),
            pl.BlockSpec((None, None, 1, d), lambda i: (l, 1, 0, 0))],
        out_specs=pl.BlockSpec((tm, d), lambda i: (i, 0)),
        out_shape=jax.ShapeDtypeStruct((m, d), f32),
        compiler_params=_cp(1), name="out_proj",
    )(x, *outs, w, mods, ln_g, ln_b)


CONV_ROWS = 128
CONV_PAD = 16
CONV_WIN = 256


def _conv_body(x_ref, w_ref, b_ref, o_ref, pad_ref, *, t, n_q, n_l2):
    cb = pl.program_id(1)
    c = x_ref.shape[1]
    rows = CONV_ROWS
    half = SHORT_CONV // 2
    pad_ref[0:CONV_PAD, :] = jnp.zeros((CONV_PAD, c), bf16)
    pad_ref[CONV_PAD + t:t + CONV_WIN, :] = jnp.zeros((CONV_WIN - CONV_PAD, c), bf16)
    pad_ref[CONV_PAD:CONV_PAD + t, :] = x_ref[...]
    ri = lax.broadcasted_iota(jnp.int32, (SHORT_CONV * rows, CONV_WIN), 0)
    ci = lax.broadcasted_iota(jnp.int32, (SHORT_CONV * rows, CONV_WIN), 1)
    sel = jnp.where(ci == (ri & (rows - 1)) + (ri >> 7) + (CONV_PAD - half), 1.0, 0.0).astype(bf16)
    for r0 in range(0, t, rows):
        taps = _dot(sel, pad_ref[r0:r0 + CONV_WIN, :])
        acc = b_ref[...] + taps[0:rows, :] * w_ref[0:1, :]
        for j in range(1, SHORT_CONV):
            acc += taps[j * rows:(j + 1) * rows, :] * w_ref[j:j + 1, :]
        y = _silu(acc)
        for c0 in range(0, c, LANES):
            yc = y[:, c0:c0 + LANES]
            if n_l2:
                r = lax.rsqrt(jnp.sum(yc * yc, axis=-1, keepdims=True) + 1e-6)
                yc = yc * jnp.where(cb < n_q, r * DN_DK ** -0.5, jnp.where(cb < n_l2, r, 1.0))
            o_ref[r0:r0 + rows, c0:c0 + LANES] = yc.astype(o_ref.dtype)


def _conv(p, t, col0, n_ch, w, b, n_q, n_l2):
    m = p.shape[0]
    cw = 512
    cb0 = col0 // cw
    return pl.pallas_call(
        functools.partial(_conv_body, t=t, n_q=n_q, n_l2=n_l2),
        grid=(m // t, n_ch // cw),
        in_specs=[pl.BlockSpec((t, cw), lambda i, j: (i, cb0 + j)),
                  pl.BlockSpec((SHORT_CONV, cw), lambda i, j: (0, j)),
                  pl.BlockSpec((1, cw), lambda i, j: (0, j))],
        out_specs=pl.BlockSpec((t, cw), lambda i, j: (i, j)),
        out_shape=jax.ShapeDtypeStruct((m, n_ch), bf16),
        scratch_shapes=[pltpu.VMEM((t + CONV_WIN, cw), bf16)],
        compiler_params=_cp(2), name="short_conv",
    )(p, w, b)


def _attend_transposed(k_s, vt_s, qt16, tk):
    n = 3 if tk % (3 * LANES) == 0 else 1
    ck = tk // n
    s_cur = _dot(k_s[0:ck, :], qt16)
    m = l = acc = None
    for c in range(n):
        s_next = _dot(k_s[(c + 1) * ck:(c + 2) * ck, :], qt16) if c + 1 < n else None
        mc = jnp.max(s_cur, axis=0, keepdims=True)
        m_new = mc if c == 0 else jnp.maximum(m, mc)
        p = jnp.exp2(s_cur - m_new)
        lc = jnp.sum(p, axis=0, keepdims=True)
        pv = _dot(vt_s[:, c * ck:(c + 1) * ck], p.astype(bf16))
        if c == 0:
            l, acc = lc, pv
        else:
            alpha = jnp.exp2(m - m_new)
            l = alpha * l + lc
            acc = alpha * acc + pv
        m, s_cur = m_new, s_next
    return acc, l


def _gqa_body(*refs, tc, tl, rope_q):
    if tl:
        (q_ref, kc_ref, vc_ref, kl_ref, vl_ref, cosq_ref, sinq_ref, cosk_ref, sink_ref,
         qn_ref, kn_ref, o_ref, k_s, v_s) = refs
    else:
        q_ref, kc_ref, vc_ref, qn_ref, kn_ref, o_ref, k_s, v_s = refs

    @pl.when(pl.program_id(2) == 0)
    def _():
        k_s[0:tc, :] = _rms(kc_ref[...].astype(f32), kn_ref[...]).astype(bf16)
        v_s[:, 0:tc] = vc_ref[...].astype(f32).T.astype(bf16)
        if tl:
            rb = min(512, tl)
            for r0 in range(0, tl, rb):
                kk = _rms(kl_ref[r0:r0 + rb, :].astype(f32), kn_ref[...])
                kk = _rope(kk, cosk_ref[r0:r0 + rb, :], sink_ref[r0:r0 + rb, :], GQA_DIM // 4)
                k_s[tc + r0:tc + r0 + rb, :] = kk.astype(bf16)
                v_s[:, tc + r0:tc + r0 + rb] = vl_ref[r0:r0 + rb, :].astype(f32).T.astype(bf16)

    tq = q_ref.shape[0]
    rep = GQA_HEADS // GQA_KV_HEADS
    qts = []
    for r in range(rep):
        q = _rms(q_ref[:, r * GQA_DIM:(r + 1) * GQA_DIM].astype(f32), qn_ref[...])
        if rope_q:
            q = _rope(q, cosq_ref[...], sinq_ref[...], GQA_DIM // 4)
        qts.append((q * (GQA_DIM ** -0.5 * LOG2E)).T)
    acc, l = _attend_transposed(k_s, v_s, jnp.concatenate(qts, axis=1).astype(bf16), tc + tl)
    o_t = acc * (1.0 / l)
    for r in range(rep):
        o_ref[:, r * GQA_DIM:(r + 1) * GQA_DIM] = o_t[:, r * tq:(r + 1) * tq].T.astype(o_ref.dtype)


def _gqa(pq, tq_seq, pc, tc, pl_, tl, rope, qn, kn):
    m = pq.shape[0]
    nb = m // tq_seq
    tq = min(512, tq_seq)
    rep = GQA_HEADS // GQA_KV_HEADS
    qw = rep * GQA_DIM
    qb0 = COL_GQA // qw
    kb0 = (COL_GQA + GQA_HEADS * GQA_DIM) // GQA_DIM
    vb0 = kb0 + GQA_KV_HEADS
    kv_spec = lambda t, b0: pl.BlockSpec((t, GQA_DIM), lambda b, g, i: (b, b0 + g))
    vec = pl.BlockSpec((1, GQA_DIM), lambda b, g, i: (0, 0))
    nq = tq_seq // tq
    in_specs = [pl.BlockSpec((tq, qw), lambda b, g, i: (b * nq + i, qb0 + g)), kv_spec(tc, kb0), kv_spec(tc, vb0)]
    args = [pq, pc, pc]
    if tl:
        cos, sin = rope
        in_specs += [kv_spec(tl, kb0), kv_spec(tl, vb0),
                     pl.BlockSpec((tq, GQA_DIM), lambda b, g, i: (i, 0)), pl.BlockSpec((tq, GQA_DIM), lambda b, g, i: (i, 0)),
                     pl.BlockSpec((tl, GQA_DIM), lambda b, g, i: (0, 0)), pl.BlockSpec((tl, GQA_DIM), lambda b, g, i: (0, 0))]
        args += [pl_, pl_, cos, sin, cos, sin]
    in_specs += [vec, vec]
    args += [qn, kn]
    return pl.pallas_call(
        functools.partial(_gqa_body, tc=tc, tl=tl, rope_q=bool(tl)),
        grid=(nb, GQA_KV_HEADS, nq),
        in_specs=in_specs,
        out_specs=pl.BlockSpec((tq, qw), lambda b, g, i: (b * nq + i, g)),
        out_shape=jax.ShapeDtypeStruct((m, GQA_W), bf16),
        scratch_shapes=[pltpu.VMEM((tc + tl, GQA_DIM), bf16), pltpu.VMEM((GQA_DIM, tc + tl), bf16)],
        compiler_params=_cp(3), name="gqa_attention",
    )(*args)


def _diff_body(*refs, tc, tl, lam_init):
    if tl:
        (q_ref, kc_ref, vc_ref, kl_ref, vl_ref, cosq_ref, sinq_ref, cosk_ref, sink_ref,
         lam_ref, ng_ref, o_ref, k_s, v_s) = refs
    else:
        q_ref, kc_ref, vc_ref, lam_ref, ng_ref, o_ref, k_s, v_s = refs

    @pl.when(pl.program_id(2) == 0)
    def _():
        k_s[0:tc, :] = kc_ref[...]
        v_s[:, 0:tc] = vc_ref[...].astype(f32).T.astype(bf16)
        if tl:
            rb = min(512, tl)
            for r0 in range(0, tl, rb):
                kk = _rope(kl_ref[r0:r0 + rb, :].astype(f32), cosk_ref[r0:r0 + rb, :], sink_ref[r0:r0 + rb, :],
                           DIFF_DIM // 4)
                k_s[tc + r0:tc + r0 + rb, :] = kk.astype(bf16)
                v_s[:, tc + r0:tc + r0 + rb] = vl_ref[r0:r0 + rb, :].astype(f32).T.astype(bf16)

    lv = lam_ref[...]
    lam = (jnp.exp(jnp.sum(lv[0:1, :] * lv[1:2, :], axis=-1, keepdims=True))
           - jnp.exp(jnp.sum(lv[2:3, :] * lv[3:4, :], axis=-1, keepdims=True)) + lam_init)
    q = q_ref[...].astype(f32)
    if tl:
        q = _rope(q, cosq_ref[...], sinq_ref[...], DIFF_DIM // 4)
    tq = q.shape[0]
    qt = (q * (DIFF_DIM ** -0.5 * LOG2E)).T
    row = lax.broadcasted_iota(jnp.int32, qt.shape, 0)
    qt16 = jnp.concatenate([jnp.where(row < DIFF_DIM, qt, 0.0), jnp.where(row >= DIFF_DIM, qt, 0.0)],
                           axis=1).astype(bf16)
    acc, l = _attend_transposed(k_s, v_s, qt16, tc + tl)
    o_t = acc * (1.0 / l)
    o = (o_t[:, 0:tq] - lam * o_t[:, tq:2 * tq]).T
    o_ref[...] = (_rms(o, ng_ref[...]) * (1.0 - lam_init)).astype(o_ref.dtype)


def _diff(pq, tq_seq, pc, tc, pl_, tl, rope, lam_vecs, ng, lam_init):
    m = pq.shape[0]
    nb = m // tq_seq
    tq = min(512, tq_seq)
    nq = tq_seq // tq
    w = DIFF_VDIM
    qb0 = COL_DIFF // w
    kb0 = qb0 + DIFF_HEADS
    vb0 = kb0 + DIFF_HEADS
    kv_spec = lambda t, b0: pl.BlockSpec((t, w), lambda b, h, i: (b, b0 + h))
    in_specs = [pl.BlockSpec((tq, w), lambda b, h, i: (b * nq + i, qb0 + h)), kv_spec(tc, kb0), kv_spec(tc, vb0)]
    args = [pq, pc, pc]
    if tl:
        cos, sin = rope
        in_specs += [kv_spec(tl, kb0), kv_spec(tl, vb0),
                     pl.BlockSpec((tq, w), lambda b, h, i: (i, 0)), pl.BlockSpec((tq, w), lambda b, h, i: (i, 0)),
                     pl.BlockSpec((tl, w), lambda b, h, i: (0, 0)), pl.BlockSpec((tl, w), lambda b, h, i: (0, 0))]
        args += [pl_, pl_, cos, sin, cos, sin]
    in_specs += [pl.BlockSpec((4, DIFF_DIM), lambda b, h, i: (0, 0)), pl.BlockSpec((1, w), lambda b, h, i: (0, 0))]
    args += [lam_vecs, ng]
    return pl.pallas_call(
        functools.partial(_diff_body, tc=tc, tl=tl, lam_init=lam_init),
        grid=(nb, DIFF_HEADS, nq),
        in_specs=in_specs,
        out_specs=pl.BlockSpec((tq, w), lambda b, h, i: (b * nq + i, h)),
        out_shape=jax.ShapeDtypeStruct((m, DIFF_W), bf16),
        scratch_shapes=[pltpu.VMEM((tc + tl, w), bf16), pltpu.VMEM((w, tc + tl), bf16)],
        compiler_params=_cp(3), name="diff_attention",
    )(*args)


def _tri_masks(n):
    row = lax.broadcasted_iota(jnp.int32, (n, n), 0)
    col = lax.broadcasted_iota(jnp.int32, (n, n), 1)
    return row >= col, row <= col, row > col, row < col


def _dn_body(*refs, tc, tl, need_ctx):
    (qc_ref, kc_ref, vc_ref, zc_ref, gc_ref, gtc_ref,
     ql_ref, kl_ref, vl_ref, zl_ref, gl_ref, gtl_ref, cp_ref, rp_ref, ng_ref) = refs[:15]
    if need_ctx:
        oc_ref, ol_ref, o_s, s_s = refs[15:]
    else:
        ol_ref, o_s, s_s = refs[15:]
        oc_ref = None
    c = DN_CHUNK
    ge, le, gt, lt = _tri_masks(c)
    incl = (ge, le)
    strict = (gt, lt)
    ge16 = jnp.where(ge, 1.0, 0.0).astype(bf16)
    le16 = jnp.where(le, 1.0, 0.0).astype(bf16)
    tri_col = (ge16, le16)
    tri_row = (le16, ge16)
    eye = (lax.broadcasted_iota(jnp.int32, (c, c), 0) == lax.broadcasted_iota(jnp.int32, (c, c), 1)).astype(f32)
    neg_a_row = -jnp.exp(cp_ref[0:1, :])
    dtb_row = cp_ref[1:2, :]
    neg_a_col = -jnp.exp(rp_ref[:, 0:1])
    dtb_col = rp_ref[:, 1:2]
    s_s[...] = jnp.zeros_like(s_s)

    def run(q_ref, k_ref, v_ref, g_ref, gt_ref, n_chunks, base):
        def step(i, carry):
            dirs = []
            for d in range(2):
                ci = i if d == 0 else n_chunks - 1 - i
                r = pl.multiple_of(ci * c, c)
                graw = g_ref[pl.ds(r, c), :]
                dirs.append(dict(r=r, beta_all=_sigmoid(graw), gdec=neg_a_row * _softplus(graw + dtb_row),
                                 gdec_r=neg_a_col * _softplus(gt_ref[ci] + dtb_col)))
            for d, dd in enumerate(dirs):
                dd["cum"] = _dot_sel_l(tri_col[d], dd["gdec"], 3)
                dd["cum_r"] = _dot_sel_r(dd["gdec_r"], tri_row[d], 3)
                dd["total"] = jnp.sum(dd["gdec"], axis=0, keepdims=True)
            chains = []
            for d, dd in enumerate(dirs):
                for h in range(DN_HEADS):
                    lb = LANE_DN_B + d * DN_HEADS + h
                    la = LANE_DN_A + d * DN_HEADS + h
                    hs = slice(h * DN_DK, (h + 1) * DN_DK)
                    gcol = dd["cum"][:, la:la + 1]
                    ch = dict(d=d, hs=hs, r=dd["r"], si=d * DN_HEADS + h, beta=dd["beta_all"][:, lb:lb + 1], gcol=gcol,
                              tot=dd["total"][:, la:la + 1], eg=jnp.exp(gcol),
                              decay=jnp.where(incl[d], jnp.exp(jnp.where(incl[d], gcol - dd["cum_r"][la:la + 1, :], 0.0)), 0.0),
                              q16=q_ref[pl.ds(dd["r"], c), hs], k16=k_ref[pl.ds(dd["r"], c), hs],
                              vf=v_ref[pl.ds(dd["r"], c), hs].astype(f32))
                    ch["kf"] = ch["k16"].astype(f32)
                    chains.append(ch)
            for ch in chains:
                kq = _dot_nt(jnp.concatenate([ch["k16"], ch["q16"]], axis=0), ch["k16"])
                ch["kk"] = kq[0:c, :]
                ch["qk"] = (kq[c:2 * c, :] * ch["decay"]).astype(bf16)
            for ch in chains:
                ch["m"] = -jnp.where(strict[ch["d"]], ch["beta"] * ch["kk"] * ch["decay"], 0.0)
                ch["inv"] = eye + ch["m"]
            for _ in range(int(math.log2(c)) - 1):
                for ch in chains:
                    m16 = ch["m"].astype(bf16)
                    ch["m"] = _dot(m16, m16)
                for ch in chains:
                    ch["inv"] = _dot(ch["inv"].astype(bf16), (eye + ch["m"]).astype(bf16))
            for ch in chains:
                rhs = jnp.concatenate([(ch["vf"] * ch["beta"]).astype(bf16),
                                       (ch["kf"] * (ch["beta"] * ch["eg"])).astype(bf16)], axis=1)
                ch["uw"] = _dot(ch["inv"].astype(bf16), rhs)
            for ch in chains:
                ch["s"] = s_s[ch["si"]]
                lhs = jnp.concatenate([ch["uw"][:, DN_DV:DN_DV + DN_DK].astype(bf16),
                                       (ch["q16"].astype(f32) * ch["eg"]).astype(bf16)], axis=0)
                ch["ws"] = _dot(lhs, ch["s"].astype(bf16))
                ch["v16"] = (ch["uw"][:, 0:DN_DV] - ch["ws"][0:c, :]).astype(bf16)
            for ch in chains:
                ch["o"] = ch["ws"][c:2 * c, :] + _dot(ch["qk"], ch["v16"])
            for ch in chains:
                kd = (ch["kf"] * jnp.exp(ch["tot"] - ch["gcol"])).astype(bf16)
                s_s[ch["si"]] = ch["s"] * jnp.exp(ch["tot"]) + _dot_tn(kd, ch["v16"])
            for ch in chains:
                o_s[ch["d"], pl.ds(base + ch["r"], c), ch["hs"]] = ch["o"]
            return carry

        lax.fori_loop(0, n_chunks, step, 0)

    run(qc_ref, kc_ref, vc_ref, gc_ref, gtc_ref, tc // c, 0)
    run(ql_ref, kl_ref, vl_ref, gl_ref, gtl_ref, tl // c, tc)

    def finish(z_ref, out_ref, base, t):
        rb = min(256, t)
        for r0 in range(0, t, rb):
            for h in range(DN_HEADS):
                hs = slice(h * DN_DV, (h + 1) * DN_DV)
                o = o_s[0, base + r0:base + r0 + rb, hs] + o_s[1, base + r0:base + r0 + rb, hs]
                out_ref[r0:r0 + rb, hs] = (_rms(o, ng_ref[...]) * _silu(z_ref[r0:r0 + rb, hs].astype(f32))).astype(out_ref.dtype)

    finish(zl_ref, ol_ref, tc, tl)
    if need_ctx:
        finish(zc_ref, oc_ref, 0, tc)


def _deltanet(qkv_c, p_c, ps_c, gt_c, tc, qkv_l, p_l, ps_l, gt_l, tl, cparams, rparams, ng, need_ctx):
    nb = qkv_l.shape[0] // tl
    w = DN_W
    c = DN_CHUNK

    def seq_specs(t):
        return [pl.BlockSpec((t, w), lambda b: (b, 0)), pl.BlockSpec((t, w), lambda b: (b, 1)),
                pl.BlockSpec((t, w), lambda b: (b, 2)),
                pl.BlockSpec((t, w), lambda b: (b, COL_DN_Z // w)),
                pl.BlockSpec((t, LANES), lambda b: (b, 0)),
                pl.BlockSpec((None, t // c, 32, c), lambda b: (b, 0, 0, 0))]

    in_specs = seq_specs(tc) + seq_specs(tl) + [
        pl.BlockSpec((8, LANES), lambda b: (0, 0)), pl.BlockSpec((32, 2), lambda b: (0, 0)),
        pl.BlockSpec((1, DN_DV), lambda b: (0, 0))]
    out_l = jax.ShapeDtypeStruct((nb * tl, w), bf16)
    out_c = jax.ShapeDtypeStruct((nb * tc, w), bf16)
    spec_l = pl.BlockSpec((tl, w), lambda b: (b, 0))
    spec_c = pl.BlockSpec((tc, w), lambda b: (b, 0))
    res = pl.pallas_call(
        functools.partial(_dn_body, tc=tc, tl=tl, need_ctx=need_ctx),
        grid=(nb,),
        in_specs=in_specs,
        out_specs=(spec_c, spec_l) if need_ctx else spec_l,
        out_shape=(out_c, out_l) if need_ctx else out_l,
        scratch_shapes=[pltpu.VMEM((2, tc + tl, w), f32), pltpu.VMEM((2 * DN_HEADS, DN_DK, DN_DV), f32)],
        compiler_params=_cp(1), name="gated_deltanet",
    )(qkv_c, qkv_c, qkv_c, p_c, ps_c, gt_c, qkv_l, qkv_l, qkv_l, p_l, ps_l, gt_l, cparams, rparams, ng)
    return (res[1], res[0]) if need_ctx else (res, None)


def _ssd_body(*refs, tc, tl, need_ctx):
    (xc_ref, zc_ref, gc_ref, gtc_ref, xl_ref, zl_ref, gl_ref, gtl_ref,
     cp_ref, rp_ref, ex_ref, dsk_ref, ng_ref) = refs[:13]
    if need_ctx:
        oc_ref, ol_ref, y_s, s_s = refs[13:]
    else:
        ol_ref, y_s, s_s = refs[13:]
        oc_ref = None
    c = SSD_CHUNK
    ge, le, _, _ = _tri_masks(c)
    incl = (ge, le)
    ge16 = jnp.where(ge, 1.0, 0.0).astype(bf16)
    le16 = jnp.where(le, 1.0, 0.0).astype(bf16)
    tri_col = (ge16, le16)
    tri_row = (le16, ge16)
    neg_a_row = -jnp.exp(cp_ref[0:1, :])
    dtb_row = cp_ref[1:2, :]
    neg_a_col = -jnp.exp(rp_ref[:, 0:1])
    dtb_col = rp_ref[:, 1:2]
    lane = lax.broadcasted_iota(jnp.int32, (c, LANES), 1)
    lo = lane < SSD_HEADDIM
    hi = lane >= SSD_HEADDIM
    s_s[...] = jnp.zeros_like(s_s)
    hpg = SSD_HEADS // SSD_GROUPS
    gw = hpg * SSD_HEADDIM
    b0 = SSD_W
    c0 = SSD_W + SSD_GROUPS * SSD_STATE

    def run(x_ref, g_ref, gt_ref, n_chunks, base):
        def step(i, carry):
            dirs = []
            for d in range(2):
                ci = i if d == 0 else n_chunks - 1 - i
                r = pl.multiple_of(ci * c, c)
                dt = _softplus(g_ref[pl.ds(r, c), :] + dtb_row)
                dirs.append(dict(r=r, dt=dt, da=dt * neg_a_row, da_r=_softplus(gt_ref[ci] + dtb_col) * neg_a_col))
            for d, dd in enumerate(dirs):
                dd["cum"] = _dot_sel_l(tri_col[d], dd["da"], 3)
                dd["cum_r"] = _dot_sel_r(dd["da_r"], tri_row[d], 3)
                dd["total"] = jnp.sum(dd["da"], axis=0, keepdims=True)
            for d, dd in enumerate(dirs):
                stacked = jnp.concatenate([dd["dt"], dd["dt"] * jnp.exp(dd["total"] - dd["cum"]), jnp.exp(dd["cum"]),
                                           jnp.broadcast_to(jnp.exp(dd["total"]), (8, LANES))], axis=0)
                e_all = _dot_sel_r(stacked, ex_ref[d], 2)
                xs = x_ref[pl.ds(dd["r"], c), 0:SSD_W].astype(f32)
                dd["xdt"] = xs * e_all[0:c, :]
                dd["xdtw"] = (xs * e_all[c:2 * c, :]).astype(bf16)
                dd["cin_e"] = e_all[2 * c:3 * c, :]
                dd["cd_e"] = e_all[3 * c:3 * c + 1, :]
            groups = []
            for d, dd in enumerate(dirs):
                for g in range(SSD_GROUPS):
                    groups.append(dict(d=d, g=g, dd=dd, si=d * SSD_GROUPS + g, gs=slice(g * gw, (g + 1) * gw),
                                       bm=x_ref[pl.ds(dd["r"], c), b0 + g * SSD_STATE:b0 + (g + 1) * SSD_STATE],
                                       cm=x_ref[pl.ds(dd["r"], c), c0 + g * SSD_STATE:c0 + (g + 1) * SSD_STATE]))
            for gg in groups:
                gg["scores"] = _dot_nt(gg["cm"], gg["bm"])
            for gg in groups:
                gg["s"] = s_s[gg["si"]]
                gg["y_off"] = _dot(gg["cm"], gg["s"].astype(bf16)) * gg["dd"]["cin_e"][:, gg["gs"]]
            for gg in groups:
                s_s[gg["si"]] = gg["s"] * gg["dd"]["cd_e"][:, gg["gs"]] + _dot_tn(gg["bm"], gg["dd"]["xdtw"][:, gg["gs"]])
            items = []
            for gg in groups:
                d, g, dd = gg["d"], gg["g"], gg["dd"]
                for pair in range(hpg // 2):
                    ps = slice(g * gw + pair * LANES, g * gw + (pair + 1) * LANES)
                    xp = dd["xdt"][:, ps]
                    for e in range(2):
                        la = LANE_SSD_DT + d * SSD_HEADS + g * hpg + pair * 2 + e
                        seg = dd["cum"][:, la:la + 1] - dd["cum_r"][la:la + 1, :]
                        decay = jnp.where(incl[d], jnp.exp(jnp.where(incl[d], seg, 0.0)), 0.0)
                        items.append(dict(m=(gg["scores"] * decay).astype(bf16),
                                          xh=jnp.where(lo if e == 0 else hi, xp, 0.0).astype(bf16)))
                    gg.setdefault("pairs", []).append((ps, pair, len(items) - 2))
            for it in items:
                it["y"] = _dot(it["m"], it["xh"])
            for gg in groups:
                for ps, pair, k in gg["pairs"]:
                    yp = gg["y_off"][:, pair * LANES:(pair + 1) * LANES] + items[k]["y"] + items[k + 1]["y"]
                    y_s[gg["d"], pl.ds(base + gg["dd"]["r"], c), ps] = yp
            return carry

        lax.fori_loop(0, n_chunks, step, 0)

    run(xc_ref, gc_ref, gtc_ref, tc // c, 0)
    run(xl_ref, gl_ref, gtl_ref, tl // c, tc)

    def finish(x_ref, z_ref, out_ref, base, t):
        rb = min(256, t)
        gcols = SSD_W // SSD_GROUPS
        for r0 in range(0, t, rb):
            for g in range(SSD_GROUPS):
                gs = slice(g * gcols, (g + 1) * gcols)
                y = y_s[0, base + r0:base + r0 + rb, gs] + y_s[1, base + r0:base + r0 + rb, gs]
                y = (y + dsk_ref[:, gs] * x_ref[r0:r0 + rb, gs].astype(f32)) * _silu(z_ref[r0:r0 + rb, gs].astype(f32))
                out_ref[r0:r0 + rb, gs] = _rms(y, ng_ref[:, gs]).astype(out_ref.dtype)

    finish(xl_ref, zl_ref, ol_ref, tc, tl)
    if need_ctx:
        finish(xc_ref, zc_ref, oc_ref, 0, tc)


def _ssd(xbc_c, p_c, ps_c, gt_c, tc, xbc_l, p_l, ps_l, gt_l, tl, cparams, rparams, expand, d_skip, ng, need_ctx):
    nb = xbc_l.shape[0] // tl
    w = SSD_W
    c = SSD_CHUNK

    def seq_specs(t):
        return [pl.BlockSpec((t, SSD_CONV_CH), lambda b: (b, 0)),
                pl.BlockSpec((t, w), lambda b: (b, COL_SSD_Z // w)),
                pl.BlockSpec((t, LANES), lambda b: (b, 0)),
                pl.BlockSpec((None, t // c, 32, c), lambda b: (b, 0, 0, 0))]

    in_specs = seq_specs(tc) + seq_specs(tl) + [
        pl.BlockSpec((8, LANES), lambda b: (0, 0)), pl.BlockSpec((32, 2), lambda b: (0, 0)),
        pl.BlockSpec((2, LANES, w), lambda b: (0, 0, 0)),
        pl.BlockSpec((1, w), lambda b: (0, 0)), pl.BlockSpec((1, w), lambda b: (0, 0))]
    out_l = jax.ShapeDtypeStruct((nb * tl, w), bf16)
    out_c = jax.ShapeDtypeStruct((nb * tc, w), bf16)
    spec_l = pl.BlockSpec((tl, w), lambda b: (b, 0))
    spec_c = pl.BlockSpec((tc, w), lambda b: (b, 0))
    res = pl.pallas_call(
        functools.partial(_ssd_body, tc=tc, tl=tl, need_ctx=need_ctx),
        grid=(nb,),
        in_specs=in_specs,
        out_specs=(spec_c, spec_l) if need_ctx else spec_l,
        out_shape=(out_c, out_l) if need_ctx else out_l,
        scratch_shapes=[pltpu.VMEM((2, tc + tl, w), f32),
                        pltpu.VMEM((2 * SSD_GROUPS, SSD_STATE, w // SSD_GROUPS), f32)],
        compiler_params=_cp(1), name="ssd_scan",
    )(xbc_c, p_c, ps_c, gt_c, xbc_l, p_l, ps_l, gt_l, cparams, rparams, expand, d_skip, ng)
    return (res[1], res[0]) if need_ctx else (res, None)


def _rope_tables(n_tok, head_dim, reps):
    rows = n_tok // GRID_W
    row = jnp.broadcast_to(jnp.arange(rows)[:, None], (rows, GRID_W)).reshape(-1).astype(f32)
    col = jnp.broadcast_to(jnp.arange(GRID_W)[None, :], (rows, GRID_W)).reshape(-1).astype(f32)
    half = head_dim // 2
    inv = ROPE_THETA ** (-jnp.arange(0, half, 2, dtype=f32) / half)
    ar = row[:, None] * inv
    ac = col[:, None] * inv
    cos = jnp.concatenate([jnp.cos(ar), jnp.cos(ar), jnp.cos(ac), jnp.cos(ac)], axis=-1)
    sin = jnp.concatenate([-jnp.sin(ar), jnp.sin(ar), -jnp.sin(ac), jnp.sin(ac)], axis=-1)
    return jnp.tile(cos, (1, reps)), jnp.tile(sin, (1, reps))


W_IN_A_SMALL = DN_CONV_CH + DN_W
W_IN_B0 = W_IN_A_SMALL + 4 * DN_HEADS
W_IN_C_SMALL = W_IN_B0 + (GQA_HEADS + 2 * GQA_KV_HEADS) * GQA_DIM + SSD_W + SSD_CONV_CH
W_IN_D0 = W_IN_C_SMALL + 2 * SSD_HEADS
D_IN = W_IN_D0 + 4 * DIFF_HEADS * DIFF_DIM + DIFF_W


PACK_CHUNK = 512


def _pack_body(wt_ref, o_ref, s_ref):
    tr = wt_ref.shape[1]
    n1 = W_IN_A_SMALL
    n2 = n1 + (W_IN_C_SMALL - W_IN_B0)
    for dst0, src0, n in ((0, 0, n1), (n1, W_IN_B0, n2 - n1), (n2, W_IN_D0, D_MAIN - n2)):
        for c0 in range(0, n, PACK_CHUNK):
            o_ref[:, dst0 + c0:dst0 + c0 + PACK_CHUNK] = wt_ref[src0 + c0:src0 + c0 + PACK_CHUNK, :].T.astype(bf16)
    small = jnp.concatenate([wt_ref[W_IN_A_SMALL:W_IN_B0, :], wt_ref[W_IN_C_SMALL:W_IN_D0, :],
                             jnp.zeros((LANES - 32, tr), f32)], axis=0)
    s_ref[...] = small.T.astype(bf16)


def _pack_w_in(w_in):
    n_layers, d, d_in = w_in.shape
    assert d_in == D_IN
    tr = 256
    return pl.pallas_call(
        _pack_body,
        grid=(n_layers, d // tr),
        in_specs=[pl.BlockSpec((None, D_IN, tr), lambda l, i: (l, 0, i))],
        out_specs=(pl.BlockSpec((None, tr, D_MAIN), lambda l, i: (l, i, 0)),
                   pl.BlockSpec((None, tr, LANES), lambda l, i: (l, i, 0))),
        out_shape=(jax.ShapeDtypeStruct((n_layers, d, D_MAIN), bf16), jax.ShapeDtypeStruct((n_layers, d, LANES), bf16)),
        compiler_params=_cp(2), name="pack_w_in",
    )(jnp.swapaxes(w_in, 1, 2))


def _gates_rowlayout(ps, t, chunk):
    g = ps[:, 0:32]
    nb = g.shape[0] // t
    return g.reshape(nb, t // chunk, chunk, 32).transpose(0, 1, 3, 2)


def _lane_params(a_log, dt_bias, lane0):
    n = a_log.size
    out = jnp.zeros((8, LANES), f32)
    out = out.at[0, lane0:lane0 + n].set(a_log.reshape(-1))
    return out.at[1, lane0:lane0 + n].set(dt_bias.reshape(-1))


def _row_params(a_log, dt_bias, row0):
    n = a_log.size
    out = jnp.zeros((32, 2), f32)
    out = out.at[row0:row0 + n, 0].set(a_log.reshape(-1))
    return out.at[row0:row0 + n, 1].set(dt_bias.reshape(-1))


def _ssd_expand():
    lane = jnp.arange(LANES)[:, None]
    colh = (jnp.arange(SSD_W) // SSD_HEADDIM)[None, :]
    return jnp.stack([(lane == LANE_SSD_DT + d * SSD_HEADS + colh).astype(bf16) for d in range(2)])


def kernel(x, c, ctx, c_ctx, w_mod, b_mod, ln_g, ln_b, ffn_wg, ffn_wu, ffn_wd, w_in, w_out, dn_conv, dn_a_log, dn_dt_bias, dn_norm_g, gqa_q_norm, gqa_k_norm, ssd_conv, ssd_conv_b, ssd_a_log, ssd_dt_bias, ssd_d, ssd_norm_g, diff_lambda, diff_norm_g):
    nb, tl, d = x.shape
    tc = ctx.shape[1]
    n_layers = w_mod.shape[0]
    xl = x.reshape(nb * tl, d)
    xc = ctx.reshape(nb * tc, d)

    rows = -(-(nb + 1) // 8) * 8
    cvec = jnp.zeros((rows, d), f32).at[:nb].set(c).at[nb].set(c_ctx)
    mods = _modulation(cvec, w_mod, b_mod).reshape(n_layers, rows, N_MOD, d)

    wg16, wu16, wd16 = ffn_wg.astype(bf16), ffn_wu.astype(bf16), ffn_wd.astype(bf16)
    win16, win_small16 = _pack_w_in(w_in)
    wout16 = w_out.astype(bf16)
    lng = ln_g.reshape(n_layers, 3, 1, d)
    lnb = ln_b.reshape(n_layers, 3, 1, d)
    rope_gqa = _rope_tables(tl, GQA_DIM, 1)
    rope_diff = _rope_tables(tl, DIFF_DIM, 2)
    ssd_ex = _ssd_expand()
    zero_bias_dn = jnp.zeros((1, DN_CONV_CH), f32)

    lat = dict(mod_row0=0, rows_per_mod=tl)
    cx = dict(mod_row0=nb, rows_per_mod=nb * tc)

    for l in range(n_layers):
        need_ctx = l < n_layers - 1
        lam_init = 0.8 - 0.6 * math.exp(-0.3 * l)
        ffn = lambda h, grp, base, j, k: _ffn(h, mods, grp["mod_row0"], grp["rows_per_mod"], base,
                                              wg16, wu16, wd16, l, j, lng, lnb, k)
        xl = ffn(xl, lat, 0, 0, 0)
        xc = ffn(xc, cx, 0, 0, 0)

        p_l, ps_l = _inproj(xl, mods, 0, tl, win16, win_small16, l)
        p_c, ps_c = _inproj(xc, mods, nb, nb * tc, win16, win_small16, l)

        dn_qkv_l = _conv(p_l, tl, COL_DN_QKV, DN_CONV_CH, dn_conv[l], zero_bias_dn, 1, 2)
        dn_qkv_c = _conv(p_c, tc, COL_DN_QKV, DN_CONV_CH, dn_conv[l], zero_bias_dn, 1, 2)
        a_l, a_c = _deltanet(dn_qkv_c, p_c, ps_c, _gates_rowlayout(ps_c, tc, DN_CHUNK), tc,
                             dn_qkv_l, p_l, ps_l, _gates_rowlayout(ps_l, tl, DN_CHUNK), tl,
                             _lane_params(dn_a_log[l], dn_dt_bias[l], LANE_DN_A),
                             _row_params(dn_a_log[l], dn_dt_bias[l], LANE_DN_A),
                             dn_norm_g[l].reshape(1, DN_DV), need_ctx)
        qn = gqa_q_norm[l].reshape(1, GQA_DIM)
        kn = gqa_k_norm[l].reshape(1, GQA_DIM)
        b_l = _gqa(p_l, tl, p_c, tc, p_l, tl, rope_gqa, qn, kn)
        b_c = _gqa(p_c, tc, p_c, tc, None, 0, None, qn, kn) if need_ctx else None
        ssd_xbc_l = _conv(p_l, tl, COL_SSD_XBC, SSD_CONV_CH, ssd_conv[l], ssd_conv_b[l].reshape(1, -1), 0, 0)
        ssd_xbc_c = _conv(p_c, tc, COL_SSD_XBC, SSD_CONV_CH, ssd_conv[l], ssd_conv_b[l].reshape(1, -1), 0, 0)
        c_l, c_c = _ssd(ssd_xbc_c, p_c, ps_c, _gates_rowlayout(ps_c, tc, SSD_CHUNK), tc,
                        ssd_xbc_l, p_l, ps_l, _gates_rowlayout(ps_l, tl, SSD_CHUNK), tl,
                        _lane_params(ssd_a_log[l], ssd_dt_bias[l], LANE_SSD_DT),
                        _row_params(ssd_a_log[l], ssd_dt_bias[l], LANE_SSD_DT),
                        ssd_ex, jnp.repeat(ssd_d[l], SSD_HEADDIM).reshape(1, SSD_W),
                        ssd_norm_g[l].reshape(1, SSD_W), need_ctx)
        ng = diff_norm_g[l].reshape(1, DIFF_VDIM)
        d_l = _diff(p_l, tl, p_c, tc, p_l, tl, rope_diff, diff_lambda[l], ng, lam_init)
        d_c = _diff(p_c, tc, p_c, tc, None, 0, None, diff_lambda[l], ng, lam_init) if need_ctx else None

        xl = _outproj(xl, (a_l, b_l, c_l, d_l), wout16, mods, 0, tl, l, lng, lnb)
        xl = ffn(xl, lat, 6, 1, 2)
        if need_ctx:
            xc = _outproj(xc, (a_c, b_c, c_c, d_c), wout16, mods, nb, nb * tc, l, lng, lnb)
            xc = ffn(xc, cx, 6, 1, 2)
    return xl.reshape(nb, tl, d)
```
